```python
import math
import jax, jax.numpy as jnp
from jax import lax
import numpy as np

D_MODEL = 2048
BATCH = 4
SEQ = 8192
DEPTH = 4
DEC_BATCH = 8
DEC_SEQ = 16
PAST_LEN = 4096

CHUNK = 64
EPS = 1e-6
ROPE_BASE = 10000.0
RET_HEADS = 6
RET_DK = 64
RET_DV = 128
GDN_HEADS = 6
GDN_DK = 128
GDN_DV = 128
CONV_WIDTH = 4
GMLP_GROUPS = 4
GMLP_DIM = 128
GMLP_CHUNK = 128
D_FF = 5632

RET_QK = RET_HEADS * RET_DK
RET_V = RET_HEADS * RET_DV
GDN_QK = GDN_HEADS * GDN_DK
GDN_V = GDN_HEADS * GDN_DV
GDN_CONV_CH = 2 * GDN_QK + GDN_V
GMLP_W = GMLP_GROUPS * GMLP_DIM
D_MIX = RET_V + GDN_V + GMLP_W
IN_SPLITS = (RET_QK, RET_QK, RET_V, RET_V, GDN_CONV_CH, GDN_V, GDN_HEADS, GDN_HEADS, GMLP_W, GMLP_W)
D_IN = RET_QK * 2 + RET_V * 2 + GDN_CONV_CH + GDN_V + GDN_HEADS * 2 + GMLP_W * 2

kernel_name = 'hybrid_streaming_retention_gdn_gmlp_step'


def _split_points(sizes):
    pts, acc = [], 0
    for s in sizes[:-1]:
        acc += s
        pts.append(acc)
    return pts


def _rmsnorm(x, w):
    xf = x.astype(jnp.float32)
    y = xf * lax.rsqrt(jnp.mean(xf * xf, axis=-1, keepdims=True) + EPS)
    return (y * w.astype(jnp.float32)).astype(x.dtype)


def _head_layernorm(x, w):
    xf = x.astype(jnp.float32)
    mu = jnp.mean(xf, axis=-1, keepdims=True)
    xc = xf - mu
    var = jnp.mean(xc * xc, axis=-1, keepdims=True)
    return xc * lax.rsqrt(var + EPS) * w.reshape(x.shape[-2], x.shape[-1]).astype(jnp.float32)


def _l2norm(x):
    xf = x.astype(jnp.float32)
    return xf * lax.rsqrt(jnp.sum(xf * xf, axis=-1, keepdims=True) + EPS)


def _swiglu(x, w_up, w_down):
    g, u = jnp.split(x @ w_up, 2, axis=-1)
    return (jax.nn.silu(g) * u) @ w_down


def _rope(x, pos):
    half = x.shape[-1] // 2
    inv = jnp.power(ROPE_BASE, -jnp.arange(half, dtype=jnp.float32) / half)
    ang = pos.astype(jnp.float32)[:, None] * inv[None, :]
    cos = jnp.cos(ang)[None, :, None, :]
    sin = jnp.sin(ang)[None, :, None, :]
    xf = x.astype(jnp.float32)
    x1, x2 = xf[..., :half], xf[..., half:]
    return jnp.concatenate([x1 * cos - x2 * sin, x2 * cos + x1 * sin], axis=-1)


def _retention(q, k, v, s0):
    B, T, H, dk = q.shape
    dv = v.shape[-1]
    c = CHUNK if T % CHUNK == 0 else T
    n = T // c
    log_gamma = jnp.log1p(-jnp.power(2.0, -5.0 - jnp.arange(H, dtype=jnp.float32)))
    idx = jnp.arange(c, dtype=jnp.float32)
    diff = idx[:, None] - idx[None, :]
    causal = diff >= 0
    dmat = jnp.where(causal[None], jnp.exp(jnp.where(causal, diff, 0.0)[None] * log_gamma[:, None, None]), 0.0)
    qc = q.reshape(B, n, c, H, dk)
    kc = k.reshape(B, n, c, H, dk)
    vc = v.reshape(B, n, c, H, dv)
    scores = jnp.einsum('bnihd,bnjhd->bnhij', qc, kc) * dmat
    o_intra = jnp.einsum('bnhij,bnjhe->bnihe', scores, vc)
    kdec = jnp.exp(log_gamma[None, :] * (c - 1 - idx)[:, None])
    kv = jnp.einsum('bnjhd,bnjhe->nbhde', kc * kdec[:, :, None], vc)
    gamma_c = jnp.exp(c * log_gamma)[None, :, None, None]

    def step(s, kv_n):
        return gamma_c * s + kv_n, s

    s_final, s_prev = lax.scan(step, s0.astype(jnp.float32), kv)
    qdec = jnp.exp(log_gamma[None, :] * (idx + 1.0)[:, None])
    o_inter = jnp.einsum('bnihd,nbhde->bnihe', qc * qdec[:, :, None], s_prev)
    return (o_intra + o_inter).reshape(B, T, H, dv), s_final.astype(s0.dtype)


def _gated_delta(q, k, v, log_a, beta, s0):
    B, T, H, dk = q.shape
    dv = v.shape[-1]
    c = CHUNK if T % CHUNK == 0 else T
    n = T // c

    def to_chunks(a):
        return a.reshape(B, n, c, H, a.shape[-1]).transpose(0, 1, 3, 2, 4)

    qc, kc, vc = to_chunks(q), to_chunks(k), to_chunks(v)
    gc = jnp.cumsum(log_a.reshape(B, n, c, H), axis=2).transpose(0, 1, 3, 2)
    bc = beta.reshape(B, n, c, H).transpose(0, 1, 3, 2)
    tri = jnp.tril(jnp.ones((c, c), dtype=bool))
    strict = jnp.tril(jnp.ones((c, c), dtype=bool), -1)
    diff = gc[..., :, None] - gc[..., None, :]
    decay = jnp.exp(jnp.where(tri, diff, -jnp.inf))
    kk = jnp.einsum('bnhid,bnhjd->bnhij', kc, kc)
    lmat = bc[..., None] * kk * jnp.where(strict, decay, 0.0)
    rhs = jnp.concatenate([bc[..., None] * vc, (bc * jnp.exp(gc))[..., None] * kc], axis=-1)
    sol = lax.linalg.triangular_solve(lmat, rhs, left_side=True, lower=True, unit_diagonal=True)
    uv, wk = sol[..., :dv], sol[..., dv:]
    aqk = jnp.einsum('bnhid,bnhjd->bnhij', qc, kc) * decay
    qg = qc * jnp.exp(gc)[..., None]
    kd = kc * jnp.exp(gc[..., -1:] - gc)[..., None]
    gl = jnp.exp(gc[..., -1])
    xs = (jnp.moveaxis(uv, 1, 0), jnp.moveaxis(wk, 1, 0), jnp.moveaxis(aqk, 1, 0),
          jnp.moveaxis(qg, 1, 0), jnp.moveaxis(kd, 1, 0), jnp.moveaxis(gl, 1, 0))

    def step(s, inp):
        uv_n, w_n, aqk_n, qg_n, kd_n, gl_n = inp
        u = uv_n - jnp.einsum('bhcd,bhde->bhce', w_n, s)
        o = jnp.einsum('bhcd,bhde->bhce', qg_n, s) + jnp.einsum('bhij,bhje->bhie', aqk_n, u)
        s = s * gl_n[..., None, None] + jnp.einsum('bhcd,bhce->bhde', kd_n, u)
        return s, o

    s_final, o = lax.scan(step, s0.astype(jnp.float32), xs)
    return o.transpose(1, 0, 3, 2, 4).reshape(B, T, H, dv), s_final.astype(s0.dtype)


def _causal_conv(x, buf, w):
    T = x.shape[1]
    xp = jnp.concatenate([buf.astype(x.dtype), x], axis=1)
    y = sum(xp[:, i:i + T] * w[i] for i in range(CONV_WIDTH))
    return jax.nn.silu(y), xp[:, T:]


def _chunk_gmlp(u, v, w_s, b_s, norm_w):
    B, T, _ = v.shape
    vg = _head_layernorm(v.reshape(B, T, GMLP_GROUPS, GMLP_DIM), norm_w)
    c = GMLP_CHUNK if T % GMLP_CHUNK == 0 else T
    n = T // c
    ws = jnp.tril(w_s[:, :c, :c].astype(jnp.float32))
    z = jnp.einsum('gij,bnjgd->bnigd', ws, vg.reshape(B, n, c, GMLP_GROUPS, GMLP_DIM))
    z = z + b_s[:, :c].T.astype(jnp.float32)[None, None, :, :, None]
    out = u.astype(jnp.float32) * z.reshape(B, T, GMLP_W)
    return out.astype(u.dtype), vg.reshape(B, T, GMLP_W).astype(u.dtype)


def _mixer(h, pos, l, p, ret_s0, gdn_s0, conv_buf):
    B, T, _ = h.shape
    proj = h @ p['w_in'][l]
    rq, rk, rv, rg, gqkv, gz, ga, gb, cu, cv = jnp.split(proj, _split_points(IN_SPLITS), axis=-1)
    rq = _rope(rq.reshape(B, T, RET_HEADS, RET_DK), pos)
    rk = _rope(rk.reshape(B, T, RET_HEADS, RET_DK), pos) * (RET_DK ** -0.5)
    ro, ret_s = _retention(rq, rk, rv.reshape(B, T, RET_HEADS, RET_DV).astype(jnp.float32), ret_s0)
    ro = _head_layernorm(ro, p['ret_gn_w'][l]).reshape(B, T, RET_V) * jax.nn.silu(rg.astype(jnp.float32))
    gqkv, conv_new = _causal_conv(gqkv, conv_buf, p['gdn_conv_w'][l])
    gq, gk, gv = jnp.split(gqkv, [GDN_QK, 2 * GDN_QK], axis=-1)
    gq = _l2norm(gq.reshape(B, T, GDN_HEADS, GDN_DK)) * (GDN_DK ** -0.5)
    gk = _l2norm(gk.reshape(B, T, GDN_HEADS, GDN_DK))
    log_a = -jnp.exp(p['gdn_a_log'][l].astype(jnp.float32)) * jax.nn.softplus(ga.astype(jnp.float32) + p['gdn_dt_bias'][l].astype(jnp.float32))
    beta = jax.nn.sigmoid(gb.astype(jnp.float32))
    go, gdn_s = _gated_delta(gq, gk, gv.reshape(B, T, GDN_HEADS, GDN_DV).astype(jnp.float32), log_a, beta, gdn_s0)
    go = go * lax.rsqrt(jnp.mean(go * go, axis=-1, keepdims=True) + EPS) * p['gdn_norm_w'][l].astype(jnp.float32)
    go = go * jax.nn.silu(gz.reshape(B, T, GDN_HEADS, GDN_DV).astype(jnp.float32))
    go = go.reshape(B, T, GDN_V)
    co, vn = _chunk_gmlp(cu, cv, p['gmlp_ws'][l], p['gmlp_bs'][l], p['gmlp_norm_w'][l])
    cat = jnp.concatenate([ro.astype(h.dtype), go.astype(h.dtype), co], axis=-1)
    return cat @ p['w_out'][l], ret_s, gdn_s, conv_new, vn


def _trunk(x, pos, ret_states, gdn_states, conv_states, p):
    new_ret, new_gdn, new_conv, new_v = [], [], [], []
    for l in range(DEPTH):
        x = x + 0.5 * _swiglu(_rmsnorm(x, p['norm_ffn1_w'][l]), p['ffn1_w_up'][l], p['ffn1_w_down'][l])
        mix, rs, gs, cs, vn = _mixer(_rmsnorm(x, p['norm_mix_w'][l]), pos, l, p,
                                     ret_states[l], gdn_states[l], conv_states[l])
        x = x + mix
        x = x + 0.5 * _swiglu(_rmsnorm(x, p['norm_ffn2_w'][l]), p['ffn2_w_up'][l], p['ffn2_w_down'][l])
        new_ret.append(rs)
        new_gdn.append(gs)
        new_conv.append(cs)
        new_v.append(vn)
    y = _rmsnorm(x, p['final_norm_w'])
    return y, jnp.stack(new_ret), jnp.stack(new_gdn), jnp.stack(new_conv), jnp.stack(new_v)


def setup_inputs(seed: int = 0) -> dict:
    key = jax.random.key(seed)
    ks = jax.random.split(key, 24)
    f32 = jnp.float32

    def nrm(k, shape, scale):
        return jax.random.normal(k, shape, f32) * scale

    def gain(k, shape):
        return 1.0 + 0.02 * jax.random.normal(k, shape, f32)

    dt = jnp.exp(jax.random.uniform(ks[13], (DEPTH, GDN_HEADS), f32, math.log(1e-3), math.log(1e-1)))
    return {
        'x_prompt': nrm(ks[0], (BATCH, SEQ, D_MODEL), 1.0),
        'x_sample': nrm(ks[1], (DEC_BATCH, DEC_SEQ, D_MODEL), 1.0),
        'state_ret': nrm(ks[2], (DEPTH, DEC_BATCH, RET_HEADS, RET_DK, RET_DV), 0.5),
        'state_gdn': nrm(ks[3], (DEPTH, DEC_BATCH, GDN_HEADS, GDN_DK, GDN_DV), 0.1),
        'state_conv': nrm(ks[4], (DEPTH, DEC_BATCH, CONV_WIDTH - 1, GDN_CONV_CH), 1.0),
        'norm_ffn1_w': gain(ks[5], (DEPTH, D_MODEL)),
        'ffn1_w_up': nrm(ks[6], (DEPTH, D_MODEL, 2 * D_FF), D_MODEL ** -0.5),
        'ffn1_w_down': nrm(ks[7], (DEPTH, D_FF, D_MODEL), D_FF ** -0.5),
        'norm_mix_w': gain(ks[8], (DEPTH, D_MODEL)),
        'w_in': nrm(ks[9], (DEPTH, D_MODEL, D_IN), D_MODEL ** -0.5),
        'ret_gn_w': gain(ks[10], (DEPTH, RET_V)),
        'gdn_conv_w': nrm(ks[11], (DEPTH, CONV_WIDTH, GDN_CONV_CH), CONV_WIDTH ** -0.5),
        'gdn_a_log': jnp.log(jax.random.uniform(ks[12], (DEPTH, GDN_HEADS), f32, 1.0, 16.0)),
        'gdn_dt_bias': jnp.log(jnp.expm1(dt)),
        'gdn_norm_w': gain(ks[14], (DEPTH, GDN_DV)),
        'gmlp_ws': nrm(ks[15], (DEPTH, GMLP_GROUPS, GMLP_CHUNK, GMLP_CHUNK), GMLP_CHUNK ** -0.5),
        'gmlp_bs': gain(ks[16], (DEPTH, GMLP_GROUPS, GMLP_CHUNK)),
        'gmlp_norm_w': gain(ks[17], (DEPTH, GMLP_W)),
        'w_out': nrm(ks[18], (DEPTH, D_MIX, D_MODEL), D_MIX ** -0.5),
        'norm_ffn2_w': gain(ks[19], (DEPTH, D_MODEL)),
        'ffn2_w_up': nrm(ks[20], (DEPTH, D_MODEL, 2 * D_FF), D_MODEL ** -0.5),
        'ffn2_w_down': nrm(ks[21], (DEPTH, D_FF, D_MODEL), D_FF ** -0.5),
        'final_norm_w': gain(ks[22], (D_MODEL,)),
    }


def reference(x_prompt, x_sample, state_ret, state_gdn, state_conv, norm_ffn1_w, ffn1_w_up, ffn1_w_down,
              norm_mix_w, w_in, ret_gn_w, gdn_conv_w, gdn_a_log, gdn_dt_bias, gdn_norm_w, gmlp_ws, gmlp_bs,
              gmlp_norm_w, w_out, norm_ffn2_w, ffn2_w_up, ffn2_w_down, final_norm_w):
    p = {'norm_ffn1_w': norm_ffn1_w, 'ffn1_w_up': ffn1_w_up, 'ffn1_w_down': ffn1_w_down,
         'norm_mix_w': norm_mix_w, 'w_in': w_in, 'ret_gn_w': ret_gn_w, 'gdn_conv_w': gdn_conv_w,
         'gdn_a_log': gdn_a_log, 'gdn_dt_bias': gdn_dt_bias, 'gdn_norm_w': gdn_norm_w,
         'gmlp_ws': gmlp_ws, 'gmlp_bs': gmlp_bs, 'gmlp_norm_w': gmlp_norm_w, 'w_out': w_out,
         'norm_ffn2_w': norm_ffn2_w, 'ffn2_w_up': ffn2_w_up, 'ffn2_w_down': ffn2_w_down,
         'final_norm_w': final_norm_w}
    bp, tp = x_prompt.shape[0], x_prompt.shape[1]
    ts = x_sample.shape[1]
    ret0 = jnp.zeros((DEPTH, bp, RET_HEADS, RET_DK, RET_DV), jnp.float32)
    gdn0 = jnp.zeros((DEPTH, bp, GDN_HEADS, GDN_DK, GDN_DV), jnp.float32)
    conv0 = jnp.zeros((DEPTH, bp, CONV_WIDTH - 1, GDN_CONV_CH), x_prompt.dtype)
    pos_p = jnp.arange(tp, dtype=jnp.int32)
    y_prompt, ret_p, gdn_p, conv_p, _ = _trunk(x_prompt, pos_p, ret0, gdn0, conv0, p)
    pos_s = PAST_LEN + jnp.arange(ts, dtype=jnp.int32)
    y_sample, ret_s, gdn_s, conv_s, gmlp_v_s = _trunk(x_sample, pos_s, state_ret, state_gdn, state_conv, p)
    return (y_prompt, y_sample, ret_p, ret_s, gdn_p, gdn_s, conv_p, conv_s, gmlp_v_s)
```

```python
import functools
import math

import jax
import jax.numpy as jnp
from jax import lax
from jax.experimental import pallas as pl
from jax.experimental.pallas import tpu as pltpu

F32 = jnp.float32
BF16 = jnp.bfloat16

D_MODEL = 2048
DEPTH = 4
PAST_LEN = 4096
EPS = 1e-6
ROPE_BASE = 10000.0
RET_HEADS = 6
RET_DK = 64
RET_DV = 128
GDN_HEADS = 6
GDN_DK = 128
GDN_DV = 128
CONV_WIDTH = 4
GMLP_GROUPS = 4
GMLP_DIM = 128
GMLP_CHUNK = 128
GDN_CHUNK = 64
D_FF = 5632

RET_QK = RET_HEADS * RET_DK
RET_V = RET_HEADS * RET_DV
GDN_QK = GDN_HEADS * GDN_DK
GDN_V = GDN_HEADS * GDN_DV
GDN_CONV_CH = 2 * GDN_QK + GDN_V
GMLP_W = GMLP_GROUPS * GMLP_DIM
D_MIX = RET_V + GDN_V + GMLP_W
D_IN = RET_QK * 2 + RET_V * 2 + GDN_CONV_CH + GDN_V + GDN_HEADS * 2 + GMLP_W * 2

LANES = 128

OFF_RQ = 0
OFF_RK = OFF_RQ + RET_QK
OFF_RV = OFF_RK + RET_QK
OFF_RG = OFF_RV + RET_V
OFF_GQKV = OFF_RG + RET_V
OFF_GZ = OFF_GQKV + GDN_CONV_CH
OFF_CU = OFF_GZ + GDN_V
OFF_CV = OFF_CU + GMLP_W
OFF_AB = OFF_CV + GMLP_W
D_PROJ = 6656

VMEM_LIMIT = 56 * 1024 * 1024


def _cparams(sem):
    return pltpu.CompilerParams(dimension_semantics=sem, vmem_limit_bytes=VMEM_LIMIT)


def _dot(a, b):
    return jnp.dot(a, b, preferred_element_type=F32)


def _dot_nt(a, b):
    return lax.dot_general(a, b, (((1,), (1,)), ((), ())), preferred_element_type=F32)


def _dot_tn(a, b):
    return lax.dot_general(a, b, (((0,), (0,)), ((), ())), preferred_element_type=F32)


def _silu(x):
    return x * jax.nn.sigmoid(x)


def _rms_rows(x, w):
    return x * lax.rsqrt(jnp.mean(x * x, axis=-1, keepdims=True) + EPS) * w


def _ffn_kernel(x_ref, nw_ref, wg_ref, wu_ref, wd_ref, o_ref, xn_ref):
    j = pl.program_id(1)

    @pl.when(j == 0)
    def _():
        x = x_ref[...]
        xn_ref[...] = _rms_rows(x, nw_ref[...]).astype(BF16)
        o_ref[...] = x

    xn = xn_ref[...]
    g = _dot(xn, wg_ref[...])
    u = _dot(xn, wu_ref[...])
    a = (0.5 * _silu(g) * u).astype(BF16)
    o_ref[...] += _dot(a, wd_ref[...])


def _ffn(x, nw, w_up, w_down, tm, tf):
    n, d = x.shape
    dff = w_down.shape[0]
    nf = dff // tf
    return pl.pallas_call(
        _ffn_kernel,
        grid=(n // tm, nf),
        in_specs=[
            pl.BlockSpec((tm, d), lambda i, j: (i, 0)),
            pl.BlockSpec((1, d), lambda i, j: (0, 0)),
            pl.BlockSpec((d, tf), lambda i, j: (0, j)),
            pl.BlockSpec((d, tf), lambda i, j: (0, j + nf)),
            pl.BlockSpec((tf, d), lambda i, j: (j, 0)),
        ],
        out_specs=pl.BlockSpec((tm, d), lambda i, j: (i, 0)),
        out_shape=jax.ShapeDtypeStruct((n, d), F32),
        scratch_shapes=[pltpu.VMEM((tm, d), BF16)],
        compiler_params=_cparams(("parallel", "arbitrary")),
        name="ffn",
    )(x, nw, w_up, w_up, w_down)


def _inproj_kernel(x_ref, nw_ref, w_ref, o_ref, xn_ref):
    @pl.when(pl.program_id(1) == 0)
    def _():
        xn_ref[...] = _rms_rows(x_ref[...], nw_ref[...]).astype(BF16)

    o_ref[...] = _dot(xn_ref[...], w_ref[...])


def _inproj(x, nw, w, tm, tn):
    n, d = x.shape
    dp = w.shape[1]
    return pl.pallas_call(
        _inproj_kernel,
        grid=(n // tm, dp // tn),
        in_specs=[
            pl.BlockSpec((tm, d), lambda i, j: (i, 0)),
            pl.BlockSpec((1, d), lambda i, j: (0, 0)),
            pl.BlockSpec((d, tn), lambda i, j: (0, j)),
        ],
        out_specs=pl.BlockSpec((tm, tn), lambda i, j: (i, j)),
        out_shape=jax.ShapeDtypeStruct((n, dp), F32),
        scratch_shapes=[pltpu.VMEM((tm, d), BF16)],
        compiler_params=_cparams(("parallel", "arbitrary")),
        name="inproj",
    )(x, nw, w)


def _outproj_kernel(x_ref, c_ref, w_ref, o_ref):
    o_ref[...] = x_ref[...] + _dot(c_ref[...], w_ref[...])


def _outproj(x, cat, w, tm):
    n, d = x.shape
    k = cat.shape[1]
    return pl.pallas_call(
        _outproj_kernel,
        grid=(n // tm,),
        in_specs=[
            pl.BlockSpec((tm, d), lambda i: (i, 0)),
            pl.BlockSpec((tm, k), lambda i: (i, 0)),
            pl.BlockSpec((k, d), lambda i: (0, 0)),
        ],
        out_specs=pl.BlockSpec((tm, d), lambda i: (i, 0)),
        out_shape=jax.ShapeDtypeStruct((n, d), F32),
        compiler_params=_cparams(("parallel",)),
        name="outproj",
    )(x, cat, w)


def _norm_kernel(x_ref, nw_ref, o_ref):
    o_ref[...] = _rms_rows(x_ref[...], nw_ref[...])


def _final_norm(x, nw, tm):
    n, d = x.shape
    return pl.pallas_call(
        _norm_kernel,
        grid=(n // tm,),
        in_specs=[pl.BlockSpec((tm, d), lambda i: (i, 0)), pl.BlockSpec((1, d), lambda i: (0, 0))],
        out_specs=pl.BlockSpec((tm, d), lambda i: (i, 0)),
        out_shape=jax.ShapeDtypeStruct((n, d), F32),
        compiler_params=_cparams(("parallel",)),
        name="final_norm",
    )(x, nw)


def _split3(x):
    hi = x.astype(BF16)
    r1 = x - hi.astype(F32)
    mid = r1.astype(BF16)
    lo = (r1 - mid.astype(F32)).astype(BF16)
    return hi, mid, lo


def _mixer_kernel(tt, gc_len, emit_vn,
                  proj_ref, cos_ref, sin_ref, dmat_ref, qdec_ref, kdec_ref, gam_ref,
                  alog_ref, dtb_ref, convw_ref, rgn_ref, gnw_ref, mnw_ref, ws_ref, bs_ref,
                  rs0_ref, gs0_ref, cv0_ref,
                  cat_ref, rs_ref, gs_ref, cvn_ref, *rest):
    if emit_vn:
        vn_ref, sret, sgdn, xbuf, gq_s, gk_s, gv_s, uv_s, wk_s, aqk_s, qg_s, kd_s, gcum_s = rest
    else:
        vn_ref = None
        sret, sgdn, xbuf, gq_s, gk_s, gv_s, uv_s, wk_s, aqk_s, qg_s, kd_s, gcum_s = rest
    c = gc_len
    nchunk = tt // c
    tp = gcum_s.shape[0]
    t = pl.program_id(1)
    nt = pl.num_programs(1)

    @pl.when(t == 0)
    def _():
        for h in range(RET_HEADS):
            r = (h % 2) * RET_DK
            sret[h] = jnp.zeros((2 * RET_DK, RET_DV), F32)
            sret[h, r:r + RET_DK, :] = rs0_ref[0, h]
        for h in range(GDN_HEADS):
            sgdn[h] = gs0_ref[0, h]
        xbuf[5:8, :] = cv0_ref[0]

    lane = lax.broadcasted_iota(jnp.int32, (tt, LANES), 1)

    cosf = cos_ref[...]
    sinf = sin_ref[...]
    first_half = (lane % RET_DK) < (RET_DK // 2)

    def rope(x):
        rot = jnp.where(first_half, pltpu.roll(x, LANES - RET_DK // 2, 1), pltpu.roll(x, RET_DK // 2, 1))
        return x * cosf + rot * sinf

    for j in range(RET_HEADS // 2):
        qp = rope(proj_ref[0, :, OFF_RQ + LANES * j:OFF_RQ + LANES * (j + 1)])
        kp = rope(proj_ref[0, :, OFF_RK + LANES * j:OFF_RK + LANES * (j + 1)]) * (RET_DK ** -0.5)
        for e in range(2):
            h = 2 * j + e
            km = jnp.where((lane >= RET_DK) == bool(e), kp, 0.0)
            v = proj_ref[0, :, OFF_RV + RET_DV * h:OFF_RV + RET_DV * (h + 1)].astype(BF16)
            s_prev = sret[h]
            scores = _dot_nt(qp.astype(BF16), km.astype(BF16)) * dmat_ref[h]
            o = _dot(scores.astype(BF16), v) + _dot((qp * qdec_ref[h]).astype(BF16), s_prev.astype(BF16))
            sret[h] = gam_ref[h] * s_prev + _dot_tn((km * kdec_ref[h]).astype(BF16), v)
            mu = jnp.mean(o, axis=-1, keepdims=True)
            oc = o - mu
            var = jnp.mean(oc * oc, axis=-1, keepdims=True)
            y = oc * lax.rsqrt(var + EPS) * rgn_ref[:, RET_DV * h:RET_DV * (h + 1)]
            g = proj_ref[0, :, OFF_RG + RET_DV * h:OFF_RG + RET_DV * (h + 1)]
            cat_ref[0, :, RET_DV * h:RET_DV * (h + 1)] = (y * _silu(g)).astype(BF16)

    xbuf[8:8 + tt, :] = proj_ref[0, :, OFF_GQKV:OFF_GQKV + GDN_CONV_CH]
    for p in range(GDN_CONV_CH // LANES):
        cs = slice(LANES * p, LANES * (p + 1))
        y = xbuf[5:5 + tt, cs] * convw_ref[0:1, cs]
        for i in range(1, CONV_WIDTH):
            y = y + xbuf[5 + i:5 + i + tt, cs] * convw_ref[i:i + 1, cs]
        y = _silu(y)
        kind, h = divmod(p, GDN_HEADS)
        if kind == 2:
            gv_s[h] = y
        else:
            yn = y * lax.rsqrt(jnp.sum(y * y, axis=-1, keepdims=True) + EPS)
            if kind == 0:
                gq_s[h] = yn * (GDN_DK ** -0.5)
            else:
                gk_s[h] = yn
    tail = xbuf[5 + tt:8 + tt, :]
    xbuf[5:8, :] = tail
    cvn_ref[0] = tail

    ab = proj_ref[0, :, OFF_AB:OFF_AB + LANES]
    log_a = -jnp.exp(alog_ref[...]) * jax.nn.softplus(ab + dtb_ref[...])
    beta = jax.nn.sigmoid(ab)
    if tp > tt:
        log_a = jnp.concatenate([log_a, jnp.zeros((tp - tt, LANES), F32)], axis=0)
    ri = lax.broadcasted_iota(jnp.int32, (tp, tp), 0)
    ci = lax.broadcasted_iota(jnp.int32, (tp, tp), 1)
    cum_mat = jnp.where((ri >= ci) & ((ri // c) == (ci // c)), 1.0, 0.0).astype(BF16)
    la_hi, la_mid, la_lo = _split3(log_a)
    gcum = (_dot(cum_mat, la_hi) + _dot(cum_mat, la_mid)) + _dot(cum_mat, la_lo)
    gcum_s[...] = gcum
    gcum_t = gcum.T

    rc = lax.broadcasted_iota(jnp.int32, (c, c), 0)
    cc = lax.broadcasted_iota(jnp.int32, (c, c), 1)
    tri = rc >= cc
    strict = rc > cc
    eye = jnp.where(rc == cc, 1.0, 0.0).astype(F32)

    for ch in range(nchunk):
        rows = slice(c * ch, c * (ch + 1))
        for h in range(GDN_HEADS):
            gcol = gcum[rows, h:h + 1]
            grow = gcum_t[h:h + 1, rows]
            decay = jnp.exp(jnp.where(tri, gcol - grow, -1e30))
            bcol = beta[rows, GDN_HEADS + h:GDN_HEADS + h + 1]
            q = gq_s[h, rows, :]
            k = gk_s[h, rows, :]
            v = gv_s[h, rows, :]
            kb = k.astype(BF16)
            lmat = bcol * _dot_nt(kb, kb) * jnp.where(strict, decay, 0.0)
            mp = -lmat
            tinv = eye + mp
            for _ in range(int(math.log2(c)) - 1):
                mpb = mp.astype(BF16)
                mp = _dot(mpb, mpb)
                tinv = tinv + _dot(tinv.astype(BF16), mp.astype(BF16))
            eg = jnp.exp(gcol)
            rhs = jnp.concatenate([bcol * v, (bcol * eg) * k], axis=1).astype(BF16)
            sol = _dot(tinv.astype(BF16), rhs)
            uv_s[h, rows, :] = sol[:, :GDN_DV]
            wk_s[h, rows, :] = sol[:, GDN_DV:].astype(BF16)
            aqk_s[h, rows, :] = (_dot_nt(q.astype(BF16), kb) * decay).astype(BF16)
            qg_s[h, rows, :] = (q * eg).astype(BF16)
            glast = gcum[c * (ch + 1) - 1:c * (ch + 1), h:h + 1]
            kd_s[h, rows, :] = (k * jnp.exp(glast - gcol)).astype(BF16)

    for ch in range(nchunk):
        rows = slice(c * ch, c * (ch + 1))
        for h in range(GDN_HEADS):
            s = sgdn[h]
            sb = s.astype(BF16)
            u = uv_s[h, rows, :] - _dot(wk_s[h, rows, :], sb)
            ub = u.astype(BF16)
            o = _dot(qg_s[h, rows, :], sb) + _dot(aqk_s[h, rows, :], ub)
            gl = jnp.exp(gcum_s[c * (ch + 1) - 1:c * (ch + 1), h:h + 1])
            sgdn[h] = s * gl + _dot_tn(kd_s[h, rows, :], ub)
            y = o * lax.rsqrt(jnp.mean(o * o, axis=-1, keepdims=True) + EPS) * gnw_ref[...]
            z = proj_ref[0, rows, OFF_GZ + GDN_DV * h:OFF_GZ + GDN_DV * (h + 1)]
            cat_ref[0, rows, RET_V + GDN_DV * h:RET_V + GDN_DV * (h + 1)] = (y * _silu(z)).astype(BF16)

    gl_len = ws_ref.shape[1]
    rg_ = lax.broadcasted_iota(jnp.int32, (gl_len, gl_len), 0)
    cg_ = lax.broadcasted_iota(jnp.int32, (gl_len, gl_len), 1)
    for ch in range(tt // gl_len):
        rows = slice(gl_len * ch, gl_len * (ch + 1))
        for g in range(GMLP_GROUPS):
            cs = slice(GMLP_DIM * g, GMLP_DIM * (g + 1))
            v = proj_ref[0, rows, OFF_CV + GMLP_DIM * g:OFF_CV + GMLP_DIM * (g + 1)]
            mu = jnp.mean(v, axis=-1, keepdims=True)
            vc = v - mu
            var = jnp.mean(vc * vc, axis=-1, keepdims=True)
            vg = vc * lax.rsqrt(var + EPS) * mnw_ref[:, cs]
            if emit_vn:
                vn_ref[0, rows, cs] = vg
            wtri = jnp.where(rg_ >= cg_, ws_ref[g], 0.0).astype(BF16)
            z = _dot(wtri, vg.astype(BF16)) + bs_ref[g]
            u = proj_ref[0, rows, OFF_CU + GMLP_DIM * g:OFF_CU + GMLP_DIM * (g + 1)]
            cat_ref[0, rows, RET_V + GDN_V + GMLP_DIM * g:RET_V + GDN_V + GMLP_DIM * (g + 1)] = (u * z).astype(BF16)

    @pl.when(t == nt - 1)
    def _():
        for h in range(RET_HEADS):
            r = (h % 2) * RET_DK
            rs_ref[0, h] = sret[h, r:r + RET_DK, :]
        for h in range(GDN_HEADS):
            gs_ref[0, h] = sgdn[h]


def _mixer(proj, cosf, sinf, consts, lp, rs0, gs0, cv0, tt, emit_vn):
    b, t, dp = proj.shape
    c = GDN_CHUNK if tt % GDN_CHUNK == 0 else tt
    tp = max(tt, LANES)
    dmat, qdec, kdec, gam = consts
    full = lambda a: pl.BlockSpec(a.shape, lambda i, j, _n=a.ndim: (0,) * _n)
    in_arrays = [proj, cosf, sinf, dmat, qdec, kdec, gam,
                 lp["alog"], lp["dtb"], lp["convw"], lp["rgn"], lp["gnw"], lp["mnw"], lp["ws"], lp["bs"],
                 rs0, gs0, cv0]
    in_specs = [
        pl.BlockSpec((1, tt, dp), lambda i, j: (i, j, 0)),
        pl.BlockSpec((tt, LANES), lambda i, j: (j, 0)),
        pl.BlockSpec((tt, LANES), lambda i, j: (j, 0)),
    ] + [full(a) for a in in_arrays[3:15]] + [
        pl.BlockSpec((1, RET_HEADS, RET_DK, RET_DV), lambda i, j: (i, 0, 0, 0)),
        pl.BlockSpec((1, GDN_HEADS, GDN_DK, GDN_DV), lambda i, j: (i, 0, 0, 0)),
        pl.BlockSpec((1, CONV_WIDTH - 1, GDN_CONV_CH), lambda i, j: (i, 0, 0)),
    ]
    out_shape = [
        jax.ShapeDtypeStruct((b, t, D_MIX), BF16),
        jax.ShapeDtypeStruct((b, RET_HEADS, RET_DK, RET_DV), F32),
        jax.ShapeDtypeStruct((b, GDN_HEADS, GDN_DK, GDN_DV), F32),
        jax.ShapeDtypeStruct((b, CONV_WIDTH - 1, GDN_CONV_CH), F32),
    ]
    out_specs = [
        pl.BlockSpec((1, tt, D_MIX), lambda i, j: (i, j, 0)),
        pl.BlockSpec((1, RET_HEADS, RET_DK, RET_DV), lambda i, j: (i, 0, 0, 0)),
        pl.BlockSpec((1, GDN_HEADS, GDN_DK, GDN_DV), lambda i, j: (i, 0, 0, 0)),
        pl.BlockSpec((1, CONV_WIDTH - 1, GDN_CONV_CH), lambda i, j: (i, 0, 0)),
    ]
    if emit_vn:
        out_shape.append(jax.ShapeDtypeStruct((b, t, GMLP_W), F32))
        out_specs.append(pl.BlockSpec((1, tt, GMLP_W), lambda i, j: (i, j, 0)))
    hs = (GDN_HEADS, tt, LANES)
    scratch = [
        pltpu.VMEM((RET_HEADS, 2 * RET_DK, RET_DV), F32),
        pltpu.VMEM((GDN_HEADS, GDN_DK, GDN_DV), F32),
        pltpu.VMEM((tt + 8, GDN_CONV_CH), F32),
        pltpu.VMEM(hs, F32), pltpu.VMEM(hs, F32), pltpu.VMEM(hs, F32),
        pltpu.VMEM(hs, F32),
        pltpu.VMEM(hs, BF16),
        pltpu.VMEM((GDN_HEADS, tt, c), BF16),
        pltpu.VMEM(hs, BF16), pltpu.VMEM(hs, BF16),
        pltpu.VMEM((tp, LANES), F32),
    ]
    return pl.pallas_call(
        functools.partial(_mixer_kernel, tt, c, emit_vn),
        grid=(b, t // tt),
        in_specs=in_specs,
        out_specs=out_specs,
        out_shape=out_shape,
        scratch_shapes=scratch,
        compiler_params=_cparams(("parallel", "arbitrary")),
        name="mixer",
    )(*in_arrays)


def _rope_tables(pos):
    half = RET_DK // 2
    inv = jnp.power(ROPE_BASE, -jnp.arange(half, dtype=F32) / half)
    ang = pos.astype(F32)[:, None] * inv[None, :]
    cos, sin = jnp.cos(ang), jnp.sin(ang)
    reps = LANES // RET_DK
    return jnp.tile(jnp.concatenate([cos, cos], axis=1), (1, reps)), jnp.tile(jnp.concatenate([-sin, sin], axis=1), (1, reps))


def _ret_consts(cr):
    log_gamma = jnp.log1p(-jnp.power(2.0, -5.0 - jnp.arange(RET_HEADS, dtype=F32)))
    idx = jnp.arange(cr, dtype=F32)
    diff = idx[:, None] - idx[None, :]
    causal = diff >= 0
    dmat = jnp.where(causal[None], jnp.exp(jnp.where(causal, diff, 0.0)[None] * log_gamma[:, None, None]), 0.0)
    kdec = jnp.exp(log_gamma[:, None] * (cr - 1 - idx)[None, :])[:, :, None]
    qdec = jnp.exp(log_gamma[:, None] * (idx + 1.0)[None, :])[:, :, None]
    gam = jnp.broadcast_to(jnp.exp(cr * log_gamma)[:, None, None], (RET_HEADS, 1, LANES))
    return dmat, qdec, kdec, gam


def _pad_lanes(v):
    return jnp.pad(v, (0, LANES - v.shape[0]))[None, :]


def _trunk(x, pos, rs0, gs0, cv0, p, tt, tm_ffn, tm_in, tm_out, emit_vn):
    b, t, d = x.shape
    n = b * t
    cosf, sinf = _rope_tables(pos)
    consts = _ret_consts(tt)
    gl_len = GMLP_CHUNK if t % GMLP_CHUNK == 0 else t
    x = x.reshape(n, d)
    outs = []
    for l in range(DEPTH):
        x = _ffn(x, p["norm_ffn1_w"][l][None], p["ffn1_w_up"][l], p["ffn1_w_down"][l], tm_ffn, 512)
        proj = _inproj(x, p["norm_mix_w"][l][None], p["w_in"][l], tm_in, 512)
        lp = {
            "alog": _pad_lanes(p["gdn_a_log"][l]), "dtb": _pad_lanes(p["gdn_dt_bias"][l]),
            "convw": p["gdn_conv_w"][l], "rgn": p["ret_gn_w"][l][None], "gnw": p["gdn_norm_w"][l][None],
            "mnw": p["gmlp_norm_w"][l][None], "ws": p["gmlp_ws"][l][:, :gl_len, :gl_len],
            "bs": p["gmlp_bs"][l][:, :gl_len, None],
        }
        res = _mixer(proj.reshape(b, t, -1), cosf, sinf, consts, lp, rs0[l], gs0[l], cv0[l], tt, emit_vn)
        x = _outproj(x, res[0].reshape(n, D_MIX), p["w_out"][l], tm_out)
        x = _ffn(x, p["norm_ffn2_w"][l][None], p["ffn2_w_up"][l], p["ffn2_w_down"][l], tm_ffn, 512)
        outs.append(res[1:])
    y = _final_norm(x, p["final_norm_w"][None], tm_out).reshape(b, t, d)
    return (y,) + tuple(jnp.stack([o[i] for o in outs]) for i in range(len(outs[0])))


def _prep_params(p):
    q = dict(p)
    for k in ("ffn1_w_up", "ffn1_w_down", "ffn2_w_up", "ffn2_w_down", "w_out"):
        q[k] = p[k].astype(BF16)
    w = p["w_in"]
    ab0 = 2 * RET_QK + 2 * RET_V + GDN_CONV_CH + GDN_V
    q["w_in"] = jnp.concatenate(
        [w[:, :, :ab0], w[:, :, ab0 + 2 * GDN_HEADS:], w[:, :, ab0:ab0 + 2 * GDN_HEADS],
         jnp.zeros(w.shape[:2] + (D_PROJ - D_IN,), w.dtype)], axis=-1).astype(BF16)
    return q


def kernel(x_prompt, x_sample, state_ret, state_gdn, state_conv, norm_ffn1_w, ffn1_w_up, ffn1_w_down,
           norm_mix_w, w_in, ret_gn_w, gdn_conv_w, gdn_a_log, gdn_dt_bias, gdn_norm_w, gmlp_ws, gmlp_bs,
           gmlp_norm_w, w_out, norm_ffn2_w, ffn2_w_up, ffn2_w_down, final_norm_w):
    p = _prep_params({
        "norm_ffn1_w": norm_ffn1_w, "ffn1_w_up": ffn1_w_up, "ffn1_w_down": ffn1_w_down,
        "norm_mix_w": norm_mix_w, "w_in": w_in, "ret_gn_w": ret_gn_w, "gdn_conv_w": gdn_conv_w,
        "gdn_a_log": gdn_a_log, "gdn_dt_bias": gdn_dt_bias, "gdn_norm_w": gdn_norm_w,
        "gmlp_ws": gmlp_ws, "gmlp_bs": gmlp_bs, "gmlp_norm_w": gmlp_norm_w, "w_out": w_out,
        "norm_ffn2_w": norm_ffn2_w, "ffn2_w_up": ffn2_w_up, "ffn2_w_down": ffn2_w_down,
        "final_norm_w": final_norm_w})
    bp, tp = x_prompt.shape[0], x_prompt.shape[1]
    bs, ts = x_sample.shape[0], x_sample.shape[1]
    ret0 = jnp.zeros((DEPTH, bp, RET_HEADS, RET_DK, RET_DV), F32)
    gdn0 = jnp.zeros((DEPTH, bp, GDN_HEADS, GDN_DK, GDN_DV), F32)
    conv0 = jnp.zeros((DEPTH, bp, CONV_WIDTH - 1, GDN_CONV_CH), F32)
    pos_p = jnp.arange(tp, dtype=jnp.int32)
    y_p, ret_p, gdn_p, conv_p = _trunk(x_prompt, pos_p, ret0, gdn0, conv0, p, tt=128,
                                       tm_ffn=min(512, bp * tp), tm_in=min(1024, bp * tp),
                                       tm_out=min(512, bp * tp), emit_vn=False)
    pos_s = PAST_LEN + jnp.arange(ts, dtype=jnp.int32)
    ns = bs * ts
    y_s, ret_s, gdn_s, conv_s, vn_s = _trunk(x_sample, pos_s, state_ret, state_gdn, state_conv, p, tt=ts,
                                             tm_ffn=ns, tm_in=ns, tm_out=ns, emit_vn=True)
    return (y_p, y_s, ret_p, ret_s, gdn_p, gdn_s, conv_p, conv_s, vn_s)
```

```python
import functools
import math

import jax
import jax.numpy as jnp
from jax import lax
from jax.experimental import pallas as pl
from jax.experimental.pallas import tpu as pltpu

F32 = jnp.float32
BF16 = jnp.bfloat16

D_MODEL = 2048
DEPTH = 4
PAST_LEN = 4096
EPS = 1e-6
ROPE_BASE = 10000.0
RET_HEADS = 6
RET_DK = 64
RET_DV = 128
GDN_HEADS = 6
GDN_DK = 128
GDN_DV = 128
CONV_WIDTH = 4
GMLP_GROUPS = 4
GMLP_DIM = 128
GMLP_CHUNK = 128
D_FF = 5632

RET_QK = RET_HEADS * RET_DK
RET_V = RET_HEADS * RET_DV
GDN_QK = GDN_HEADS * GDN_DK
GDN_V = GDN_HEADS * GDN_DV
GDN_CONV_CH = 2 * GDN_QK + GDN_V
GMLP_W = GMLP_GROUPS * GMLP_DIM
D_MIX = RET_V + GDN_V + GMLP_W
D_IN = RET_QK * 2 + RET_V * 2 + GDN_CONV_CH + GDN_V + GDN_HEADS * 2 + GMLP_W * 2

LANES = 128
TILE = 128
INV_BLOCK = 16

OFF_RQ = 0
OFF_RK = OFF_RQ + RET_QK
OFF_RV = OFF_RK + RET_QK
OFF_RG = OFF_RV + RET_V
OFF_GQKV = OFF_RG + RET_V
OFF_GZ = OFF_GQKV + GDN_CONV_CH
OFF_CU = OFF_GZ + GDN_V
OFF_CV = OFF_CU + GMLP_W
OFF_AB = OFF_CV + GMLP_W
D_PROJ = 6656

VMEM_LIMIT = 56 * 1024 * 1024


def _cparams(sem):
    return pltpu.CompilerParams(dimension_semantics=sem, vmem_limit_bytes=VMEM_LIMIT)


def _dot(a, b):
    return jnp.dot(a, b, preferred_element_type=F32)


def _dot_nt(a, b):
    return lax.dot_general(a, b, (((1,), (1,)), ((), ())), preferred_element_type=F32)


def _dot_tn(a, b):
    return lax.dot_general(a, b, (((0,), (0,)), ((), ())), preferred_element_type=F32)


def _silu(x):
    return x * jax.nn.sigmoid(x)


def _rms_rows(x, w):
    return x * lax.rsqrt(jnp.mean(x * x, axis=-1, keepdims=True) + EPS) * w


def _ffn_kernel(x_ref, nw_ref, wg_ref, wu_ref, wd_ref, o_ref, xn_ref):
    j = pl.program_id(1)

    @pl.when(j == 0)
    def _():
        x = x_ref[...]
        xn_ref[...] = _rms_rows(x, nw_ref[...]).astype(BF16)
        o_ref[...] = x

    xn = xn_ref[...]
    g = _dot(xn, wg_ref[...])
    u = _dot(xn, wu_ref[...])
    a = (0.5 * _silu(g) * u).astype(BF16)
    o_ref[...] += _dot(a, wd_ref[...])


def _ffn(x, nw, w_up, w_down, l, tm, tf):
    n, d = x.shape
    dff = w_down.shape[1]
    nf = dff // tf
    return pl.pallas_call(
        _ffn_kernel,
        grid=(n // tm, nf),
        in_specs=[
            pl.BlockSpec((tm, d), lambda i, j: (i, 0)),
            pl.BlockSpec((1, d), lambda i, j: (0, 0)),
            pl.BlockSpec((None, d, tf), lambda i, j: (l, 0, j)),
            pl.BlockSpec((None, d, tf), lambda i, j: (l, 0, j + nf)),
            pl.BlockSpec((None, tf, d), lambda i, j: (l, j, 0)),
        ],
        out_specs=pl.BlockSpec((tm, d), lambda i, j: (i, 0)),
        out_shape=jax.ShapeDtypeStruct((n, d), F32),
        scratch_shapes=[pltpu.VMEM((tm, d), BF16)],
        compiler_params=_cparams(("parallel", "arbitrary")),
        name="ffn",
    )(x, nw, w_up, w_up, w_down)


def _inproj_kernel(x_ref, nw_ref, w_ref, o_ref, xn_ref):
    @pl.when(pl.program_id(1) == 0)
    def _():
        xn_ref[...] = _rms_rows(x_ref[...], nw_ref[...]).astype(BF16)

    o_ref[...] = _dot(xn_ref[...], w_ref[...])


def _inproj(x, nw, w, l, tm, tn):
    n, d = x.shape
    dp = w.shape[2]
    return pl.pallas_call(
        _inproj_kernel,
        grid=(n // tm, dp // tn),
        in_specs=[
            pl.BlockSpec((tm, d), lambda i, j: (i, 0)),
            pl.BlockSpec((1, d), lambda i, j: (0, 0)),
            pl.BlockSpec((None, d, tn), lambda i, j: (l, 0, j)),
        ],
        out_specs=pl.BlockSpec((tm, tn), lambda i, j: (i, j)),
        out_shape=jax.ShapeDtypeStruct((n, dp), F32),
        scratch_shapes=[pltpu.VMEM((tm, d), BF16)],
        compiler_params=_cparams(("parallel", "arbitrary")),
        name="inproj",
    )(x, nw, w)


def _outproj_kernel(x_ref, c_ref, w_ref, o_ref):
    o_ref[...] = x_ref[...] + _dot(c_ref[...], w_ref[...])


def _outproj(x, cat, w, l, tm):
    n, d = x.shape
    k = cat.shape[1]
    return pl.pallas_call(
        _outproj_kernel,
        grid=(n // tm,),
        in_specs=[
            pl.BlockSpec((tm, d), lambda i: (i, 0)),
            pl.BlockSpec((tm, k), lambda i: (i, 0)),
            pl.BlockSpec((None, k, d), lambda i: (l, 0, 0)),
        ],
        out_specs=pl.BlockSpec((tm, d), lambda i: (i, 0)),
        out_shape=jax.ShapeDtypeStruct((n, d), F32),
        compiler_params=_cparams(("parallel",)),
        name="outproj",
    )(x, cat, w)


def _norm_kernel(x_ref, nw_ref, o_ref):
    o_ref[...] = _rms_rows(x_ref[...], nw_ref[...])


def _final_norm(x, nw, tm):
    n, d = x.shape
    return pl.pallas_call(
        _norm_kernel,
        grid=(n // tm,),
        in_specs=[pl.BlockSpec((tm, d), lambda i: (i, 0)), pl.BlockSpec((1, d), lambda i: (0, 0))],
        out_specs=pl.BlockSpec((tm, d), lambda i: (i, 0)),
        out_shape=jax.ShapeDtypeStruct((n, d), F32),
        compiler_params=_cparams(("parallel",)),
        name="final_norm",
    )(x, nw)


def _split3(x):
    hi = x.astype(BF16)
    r1 = x - hi.astype(F32)
    mid = r1.astype(BF16)
    lo = (r1 - mid.astype(F32)).astype(BF16)
    return hi, mid, lo


def _mixer_kernel(valid, emit_vn,
                  proj_ref, cos_ref, sin_ref, dmat_ref, qdec_ref, kdec_ref, gam_ref,
                  alog_ref, dtb_ref, convw_ref, rgn_ref, gnw_ref, mnw_ref, ws_ref, bs_ref,
                  rs0_ref, gs0_ref, cv0_ref,
                  cat_ref, rs_ref, gs_ref, cvn_ref, *rest):
    if emit_vn:
        vn_ref, sret, sgdn, xbuf, gq_s, gk_s, gv_s, uv_s, a1_s, a2_s = rest
    else:
        vn_ref = None
        sret, sgdn, xbuf, gq_s, gk_s, gv_s, uv_s, a1_s, a2_s = rest
    tt = TILE
    t = pl.program_id(1)
    nt = pl.num_programs(1)

    @pl.when(t == 0)
    def _():
        for h in range(RET_HEADS):
            r = (h % 2) * RET_DK
            sret[h] = jnp.zeros((2 * RET_DK, RET_DV), F32)
            sret[h, r:r + RET_DK, :] = rs0_ref[0, h]
        for h in range(GDN_HEADS):
            sgdn[h] = gs0_ref[0, h]
        xbuf[5:8, :] = cv0_ref[0]

    lane = lax.broadcasted_iota(jnp.int32, (tt, LANES), 1)
    ri = lax.broadcasted_iota(jnp.int32, (tt, tt), 0)
    ci = lax.broadcasted_iota(jnp.int32, (tt, tt), 1)
    tri = ri >= ci
    strict = ri > ci
    eye = jnp.where(ri == ci, 1.0, 0.0).astype(F32)

    cosf = cos_ref[...]
    sinf = sin_ref[...]
    first_half = (lane % RET_DK) < (RET_DK // 2)

    def rope(x):
        rot = jnp.where(first_half, pltpu.roll(x, LANES - RET_DK // 2, 1), pltpu.roll(x, RET_DK // 2, 1))
        return x * cosf + rot * sinf

    for j in range(RET_HEADS // 2):
        qp = rope(proj_ref[0, :, OFF_RQ + LANES * j:OFF_RQ + LANES * (j + 1)])
        kp = rope(proj_ref[0, :, OFF_RK + LANES * j:OFF_RK + LANES * (j + 1)]) * (RET_DK ** -0.5)
        qpb = qp.astype(BF16)
        for e in range(2):
            h = 2 * j + e
            km = jnp.where(lane >= RET_DK if e else lane < RET_DK, kp, 0.0)
            v = proj_ref[0, :, OFF_RV + RET_DV * h:OFF_RV + RET_DV * (h + 1)].astype(BF16)
            s_prev = sret[h]
            scores = _dot_nt(qpb, km.astype(BF16)) * dmat_ref[h]
            lhs = jnp.concatenate([scores.astype(BF16), (qp * qdec_ref[h]).astype(BF16)], axis=1)
            o = _dot(lhs, jnp.concatenate([v, s_prev.astype(BF16)], axis=0))
            sret[h] = gam_ref[h] * s_prev + _dot_tn((km * kdec_ref[h]).astype(BF16), v)
            mu = jnp.mean(o, axis=-1, keepdims=True)
            oc = o - mu
            var = jnp.mean(oc * oc, axis=-1, keepdims=True)
            y = oc * lax.rsqrt(var + EPS) * rgn_ref[:, RET_DV * h:RET_DV * (h + 1)]
            g = proj_ref[0, :, OFF_RG + RET_DV * h:OFF_RG + RET_DV * (h + 1)]
            cat_ref[0, :, RET_DV * h:RET_DV * (h + 1)] = (y * _silu(g)).astype(BF16)

    xbuf[8:8 + tt, :] = proj_ref[0, :, OFF_GQKV:OFF_GQKV + GDN_CONV_CH]
    for p in range(GDN_CONV_CH // LANES):
        cs = slice(LANES * p, LANES * (p + 1))
        y = xbuf[5:5 + tt, cs] * convw_ref[0:1, cs]
        for i in range(1, CONV_WIDTH):
            y = y + xbuf[5 + i:5 + i + tt, cs] * convw_ref[i:i + 1, cs]
        y = _silu(y)
        kind, h = divmod(p, GDN_HEADS)
        if kind == 2:
            gv_s[h] = y
        else:
            yn = y * lax.rsqrt(jnp.sum(y * y, axis=-1, keepdims=True) + EPS)
            if kind == 0:
                gq_s[h] = yn * (GDN_DK ** -0.5)
            else:
                gk_s[h] = yn
    tail = xbuf[5 + valid:8 + valid, :]
    xbuf[5:8, :] = tail
    cvn_ref[0] = tail

    ab = proj_ref[0, :, OFF_AB:OFF_AB + LANES]
    log_a = -jnp.exp(alog_ref[...]) * jax.nn.softplus(ab + dtb_ref[...])
    beta = jax.nn.sigmoid(ab)
    if valid < tt:
        real = lax.broadcasted_iota(jnp.int32, (tt, LANES), 0) < valid
        log_a = jnp.where(real, log_a, 0.0)
        beta = jnp.where(real, beta, 0.0)
    cum_mat = jnp.where(tri, 1.0, 0.0).astype(BF16)
    la_hi, la_mid, la_lo = _split3(log_a)
    gcum = (_dot(cum_mat, la_hi) + _dot(cum_mat, la_mid)) + _dot(cum_mat, la_lo)
    gcum_t = gcum.T

    ls = []
    for h in range(GDN_HEADS):
        gcol = gcum[:, h:h + 1]
        grow = gcum_t[h:h + 1, :]
        decay = jnp.exp(jnp.where(tri, gcol - grow, -1e30))
        bcol = beta[:, GDN_HEADS + h:GDN_HEADS + h + 1]
        kb = gk_s[h].astype(BF16)
        r = _dot_nt(jnp.concatenate([gq_s[h].astype(BF16), kb], axis=0), kb)
        a2_s[h, 0:tt, :] = (r[:tt] * decay).astype(BF16)
        ls.append(bcol * r[tt:] * jnp.where(strict, decay, 0.0))

    zero_b = jnp.zeros((tt, tt), BF16)

    def pair_dot(a2, b2):
        bb = b2.astype(BF16)
        rhs = jnp.concatenate([jnp.concatenate([bb[:, :tt], zero_b], axis=1),
                               jnp.concatenate([zero_b, bb[:, tt:]], axis=1)], axis=0)
        return _dot(a2.astype(BF16), rhs)

    def same_block(b):
        return (ri // b) == (ci // b)

    pairs = range(GDN_HEADS // 2)
    eye2 = jnp.concatenate([eye, eye], axis=1)
    l2 = [jnp.concatenate([ls[2 * j], ls[2 * j + 1]], axis=1) for j in pairs]
    diag = same_block(INV_BLOCK)
    diag2 = jnp.concatenate([diag, diag], axis=1)
    pw = [jnp.where(diag2, -l2[j], 0.0) for j in pairs]
    td = [eye2 + pw[j] for j in pairs]
    n_fac = max(int(math.ceil(math.log2(min(INV_BLOCK, valid)))), 1)
    for it in range(n_fac):
        if it == 0:
            if n_fac > 1:
                pw = [pair_dot(pw[j], pw[j]) for j in pairs]
        elif it == n_fac - 1:
            td = [td[j] + pair_dot(td[j], pw[j]) for j in pairs]
        else:
            r = [pair_dot(jnp.concatenate([pw[j], td[j]], axis=0), pw[j]) for j in pairs]
            pw = [r[j][:tt] for j in pairs]
            td = [td[j] + r[j][tt:] for j in pairs]
    blk = INV_BLOCK
    while blk < valid:
        off = same_block(2 * blk) & jnp.logical_not(same_block(blk))
        off2 = jnp.concatenate([off, off], axis=1)
        y = [pair_dot(jnp.where(off2, l2[j], 0.0), td[j]) for j in pairs]
        z = [pair_dot(td[j], y[j]) for j in pairs]
        td = [td[j] - z[j] for j in pairs]
        blk *= 2

    gls = []
    for h in range(GDN_HEADS):
        j, e = divmod(h, 2)
        tinv = td[j][:, tt * e:tt * (e + 1)]
        gcol = gcum[:, h:h + 1]
        eg = jnp.exp(gcol)
        bcol = beta[:, GDN_HEADS + h:GDN_HEADS + h + 1]
        k = gk_s[h]
        rhs = jnp.concatenate([bcol * gv_s[h], (bcol * eg) * k], axis=1).astype(BF16)
        sol = _dot(tinv.astype(BF16), rhs)
        uv_s[h] = sol[:, :GDN_DV]
        a1_s[h, 0:tt, :] = sol[:, GDN_DV:].astype(BF16)
        a1_s[h, tt:2 * tt, :] = (gq_s[h] * eg).astype(BF16)
        glast = gcum[tt - 1:tt, h:h + 1]
        a2_s[h, tt:2 * tt, :] = (k * jnp.exp(glast - gcol)).T.astype(BF16)
        gls.append(jnp.exp(glast))

    ss = [sgdn[h] for h in range(GDN_HEADS)]
    r1 = [_dot(a1_s[h], ss[h].astype(BF16)) for h in range(GDN_HEADS)]
    ub = [(uv_s[h] - r1[h][:tt]).astype(BF16) for h in range(GDN_HEADS)]
    r2 = [_dot(a2_s[h], ub[h]) for h in range(GDN_HEADS)]
    for h in range(GDN_HEADS):
        sgdn[h] = ss[h] * gls[h] + r2[h][tt:]
        o = r1[h][tt:] + r2[h][:tt]
        y = o * lax.rsqrt(jnp.mean(o * o, axis=-1, keepdims=True) + EPS) * gnw_ref[...]
        z = proj_ref[0, :, OFF_GZ + GDN_DV * h:OFF_GZ + GDN_DV * (h + 1)]
        cat_ref[0, :, RET_V + GDN_DV * h:RET_V + GDN_DV * (h + 1)] = (y * _silu(z)).astype(BF16)

    for g in range(GMLP_GROUPS):
        cs = slice(GMLP_DIM * g, GMLP_DIM * (g + 1))
        v = proj_ref[0, :, OFF_CV + GMLP_DIM * g:OFF_CV + GMLP_DIM * (g + 1)]
        mu = jnp.mean(v, axis=-1, keepdims=True)
        vc = v - mu
        var = jnp.mean(vc * vc, axis=-1, keepdims=True)
        vg = vc * lax.rsqrt(var + EPS) * mnw_ref[:, cs]
        if emit_vn:
            vn_ref[0, :, cs] = vg
        wtri = jnp.where(tri, ws_ref[g], 0.0).astype(BF16)
        z = _dot(wtri, vg.astype(BF16)) + bs_ref[g]
        u = proj_ref[0, :, OFF_CU + GMLP_DIM * g:OFF_CU + GMLP_DIM * (g + 1)]
        cat_ref[0, :, RET_V + GDN_V + GMLP_DIM * g:RET_V + GDN_V + GMLP_DIM * (g + 1)] = (u * z).astype(BF16)

    @pl.when(t == nt - 1)
    def _():
        for h in range(RET_HEADS):
            r = (h % 2) * RET_DK
            rs_ref[0, h] = sret[h, r:r + RET_DK, :]
        for h in range(GDN_HEADS):
            gs_ref[0, h] = sgdn[h]


def _mixer(proj, cosf, sinf, consts, lp, rs0, gs0, cv0, valid, emit_vn):
    b, t, dp = proj.shape
    tt = TILE
    dmat, qdec, kdec, gam = consts
    full = lambda a: pl.BlockSpec(a.shape, lambda i, j, _n=a.ndim: (0,) * _n)
    in_arrays = [proj, cosf, sinf, dmat, qdec, kdec, gam,
                 lp["alog"], lp["dtb"], lp["convw"], lp["rgn"], lp["gnw"], lp["mnw"], lp["ws"], lp["bs"],
                 rs0, gs0, cv0]
    in_specs = [
        pl.BlockSpec((1, tt, dp), lambda i, j: (i, j, 0)),
        pl.BlockSpec((tt, LANES), lambda i, j: (j, 0)),
        pl.BlockSpec((tt, LANES), lambda i, j: (j, 0)),
    ] + [full(a) for a in in_arrays[3:15]] + [
        pl.BlockSpec((1, RET_HEADS, RET_DK, RET_DV), lambda i, j: (i, 0, 0, 0)),
        pl.BlockSpec((1, GDN_HEADS, GDN_DK, GDN_DV), lambda i, j: (i, 0, 0, 0)),
        pl.BlockSpec((1, CONV_WIDTH - 1, GDN_CONV_CH), lambda i, j: (i, 0, 0)),
    ]
    out_shape = [
        jax.ShapeDtypeStruct((b, t, D_MIX), BF16),
        jax.ShapeDtypeStruct((b, RET_HEADS, RET_DK, RET_DV), F32),
        jax.ShapeDtypeStruct((b, GDN_HEADS, GDN_DK, GDN_DV), F32),
        jax.ShapeDtypeStruct((b, CONV_WIDTH - 1, GDN_CONV_CH), F32),
    ]
    out_specs = [
        pl.BlockSpec((1, tt, D_MIX), lambda i, j: (i, j, 0)),
        pl.BlockSpec((1, RET_HEADS, RET_DK, RET_DV), lambda i, j: (i, 0, 0, 0)),
        pl.BlockSpec((1, GDN_HEADS, GDN_DK, GDN_DV), lambda i, j: (i, 0, 0, 0)),
        pl.BlockSpec((1, CONV_WIDTH - 1, GDN_CONV_CH), lambda i, j: (i, 0, 0)),
    ]
    if emit_vn:
        out_shape.append(jax.ShapeDtypeStruct((b, t, GMLP_W), F32))
        out_specs.append(pl.BlockSpec((1, tt, GMLP_W), lambda i, j: (i, j, 0)))
    hs = (GDN_HEADS, tt, LANES)
    scratch = [
        pltpu.VMEM((RET_HEADS, 2 * RET_DK, RET_DV), F32),
        pltpu.VMEM((GDN_HEADS, GDN_DK, GDN_DV), F32),
        pltpu.VMEM((tt + 8, GDN_CONV_CH), F32),
        pltpu.VMEM(hs, F32), pltpu.VMEM(hs, F32), pltpu.VMEM(hs, F32),
        pltpu.VMEM(hs, F32),
        pltpu.VMEM((GDN_HEADS, 2 * tt, LANES), BF16),
        pltpu.VMEM((GDN_HEADS, 2 * tt, LANES), BF16),
    ]
    return pl.pallas_call(
        functools.partial(_mixer_kernel, valid, emit_vn),
        grid=(b, t // tt),
        in_specs=in_specs,
        out_specs=out_specs,
        out_shape=out_shape,
        scratch_shapes=scratch,
        compiler_params=_cparams(("parallel", "arbitrary")),
        name="mixer",
    )(*in_arrays)


def _pad_rows(a, rows, axis):
    pad = [(0, 0)] * a.ndim
    pad[axis] = (0, rows - a.shape[axis])
    return jnp.pad(a, pad)


def _rope_tables(pos):
    half = RET_DK // 2
    inv = jnp.power(ROPE_BASE, -jnp.arange(half, dtype=F32) / half)
    ang = pos.astype(F32)[:, None] * inv[None, :]
    cos, sin = jnp.cos(ang), jnp.sin(ang)
    reps = LANES // RET_DK
    return jnp.tile(jnp.concatenate([cos, cos], axis=1), (1, reps)), jnp.tile(jnp.concatenate([-sin, sin], axis=1), (1, reps))


def _ret_consts(cr):
    log_gamma = jnp.log1p(-jnp.power(2.0, -5.0 - jnp.arange(RET_HEADS, dtype=F32)))
    idx = jnp.arange(cr, dtype=F32)
    diff = idx[:, None] - idx[None, :]
    causal = diff >= 0
    dmat = jnp.where(causal[None], jnp.exp(jnp.where(causal, diff, 0.0)[None] * log_gamma[:, None, None]), 0.0)
    kdec = jnp.exp(log_gamma[:, None] * (cr - 1 - idx)[None, :])[:, :, None]
    qdec = jnp.exp(log_gamma[:, None] * (idx + 1.0)[None, :])[:, :, None]
    gam = jnp.broadcast_to(jnp.exp(cr * log_gamma)[:, None, None], (RET_HEADS, 1, LANES))
    dmat = _pad_rows(_pad_rows(dmat, TILE, 1), TILE, 2)
    return dmat, _pad_rows(qdec, TILE, 1), _pad_rows(kdec, TILE, 1), gam


def _pad_lanes(v):
    return jnp.pad(v, (0, LANES - v.shape[0]))[None, :]


def _trunk(x, pos, rs0, gs0, cv0, p, tm_ffn, tm_in, tm_out, emit_vn):
    b, t, d = x.shape
    n = b * t
    valid = min(t, TILE)
    tpad = -(-t // TILE) * TILE
    cosf, sinf = _rope_tables(pos)
    cosf, sinf = _pad_rows(cosf, tpad, 0), _pad_rows(sinf, tpad, 0)
    consts = _ret_consts(valid)
    x = x.reshape(n, d)
    outs = []
    for l in range(DEPTH):
        x = _ffn(x, p["norm_ffn1_w"][l][None], p["ffn1_w_up"], p["ffn1_w_down"], l, tm_ffn, 512)
        proj = _inproj(x, p["norm_mix_w"][l][None], p["w_in"], l, tm_in, 512)
        lp = {
            "alog": _pad_lanes(p["gdn_a_log"][l]), "dtb": _pad_lanes(p["gdn_dt_bias"][l]),
            "convw": p["gdn_conv_w"][l], "rgn": p["ret_gn_w"][l][None], "gnw": p["gdn_norm_w"][l][None],
            "mnw": p["gmlp_norm_w"][l][None],
            "ws": _pad_rows(_pad_rows(p["gmlp_ws"][l][:, :valid, :valid], TILE, 1), TILE, 2),
            "bs": _pad_rows(p["gmlp_bs"][l][:, :valid, None], TILE, 1),
        }
        res = _mixer(_pad_rows(proj.reshape(b, t, -1), tpad, 1), cosf, sinf, consts, lp,
                     rs0[l], gs0[l], cv0[l], valid, emit_vn)
        cat = res[0][:, :t].reshape(n, D_MIX)
        x = _outproj(x, cat, p["w_out"], l, tm_out)
        x = _ffn(x, p["norm_ffn2_w"][l][None], p["ffn2_w_up"], p["ffn2_w_down"], l, tm_ffn, 512)
        outs.append(tuple(res[1:4]) + ((res[4][:, :t],) if emit_vn else ()))
    y = _final_norm(x, p["final_norm_w"][None], tm_out).reshape(b, t, d)
    return (y,) + tuple(jnp.stack([o[i] for o in outs]) for i in range(len(outs[0])))


def _prep_params(p):
    q = dict(p)
    for k in ("ffn1_w_up", "ffn1_w_down", "ffn2_w_up", "ffn2_w_down", "w_out"):
        q[k] = p[k].astype(BF16)
    w = p["w_in"]
    ab0 = 2 * RET_QK + 2 * RET_V + GDN_CONV_CH + GDN_V
    q["w_in"] = jnp.concatenate(
        [w[:, :, :ab0], w[:, :, ab0 + 2 * GDN_HEADS:], w[:, :, ab0:ab0 + 2 * GDN_HEADS],
         jnp.zeros(w.shape[:2] + (D_PROJ - D_IN,), w.dtype)], axis=-1).astype(BF16)
    return q


def kernel(x_prompt, x_sample, state_ret, state_gdn, state_conv, norm_ffn1_w, ffn1_w_up, ffn1_w_down,
           norm_mix_w, w_in, ret_gn_w, gdn_conv_w, gdn_a_log, gdn_dt_bias, gdn_norm_w, gmlp_ws, gmlp_bs,
           gmlp_norm_w, w_out, norm_ffn2_w, ffn2_w_up, ffn2_w_down, final_norm_w):
    p = _prep_params({
        "norm_ffn1_w": norm_ffn1_w, "ffn1_w_up": ffn1_w_up, "ffn1_w_down": ffn1_w_down,
        "norm_mix_w": norm_mix_w, "w_in": w_in, "ret_gn_w": ret_gn_w, "gdn_conv_w": gdn_conv_w,
        "gdn_a_log": gdn_a_log, "gdn_dt_bias": gdn_dt_bias, "gdn_norm_w": gdn_norm_w,
        "gmlp_ws": gmlp_ws, "gmlp_bs": gmlp_bs, "gmlp_norm_w": gmlp_norm_w, "w_out": w_out,
        "norm_ffn2_w": norm_ffn2_w, "ffn2_w_up": ffn2_w_up, "ffn2_w_down": ffn2_w_down,
        "final_norm_w": final_norm_w})
    bp, tp = x_prompt.shape[0], x_prompt.shape[1]
    bs, ts = x_sample.shape[0], x_sample.shape[1]
    ret0 = jnp.zeros((DEPTH, bp, RET_HEADS, RET_DK, RET_DV), F32)
    gdn0 = jnp.zeros((DEPTH, bp, GDN_HEADS, GDN_DK, GDN_DV), F32)
    conv0 = jnp.zeros((DEPTH, bp, CONV_WIDTH - 1, GDN_CONV_CH), F32)
    pos_p = jnp.arange(tp, dtype=jnp.int32)
    y_p, ret_p, gdn_p, conv_p = _trunk(x_prompt, pos_p, ret0, gdn0, conv0, p,
                                       tm_ffn=min(512, bp * tp), tm_in=min(1024, bp * tp),
                                       tm_out=min(512, bp * tp), emit_vn=False)
    pos_s = PAST_LEN + jnp.arange(ts, dtype=jnp.int32)
    ns = bs * ts
    y_s, ret_s, gdn_s, conv_s, vn_s = _trunk(x_sample, pos_s, state_ret, state_gdn, state_conv, p,
                                             tm_ffn=ns, tm_in=ns, tm_out=ns, emit_vn=True)
    return (y_p, y_s, ret_p, ret_s, gdn_p, gdn_s, conv_p, conv_s, vn_s)
```

```python
import functools
import math

import jax
import jax.numpy as jnp
from jax import lax
from jax.experimental import pallas as pl
from jax.experimental.pallas import tpu as pltpu

F32 = jnp.float32
BF16 = jnp.bfloat16

D_MODEL = 2048
DEPTH = 4
PAST_LEN = 4096
EPS = 1e-6
ROPE_BASE = 10000.0
RET_HEADS = 6
RET_DK = 64
RET_DV = 128
GDN_HEADS = 6
GDN_DK = 128
GDN_DV = 128
CONV_WIDTH = 4
GMLP_GROUPS = 4
GMLP_DIM = 128
GMLP_CHUNK = 128
D_FF = 5632

RET_QK = RET_HEADS * RET_DK
RET_V = RET_HEADS * RET_DV
GDN_QK = GDN_HEADS * GDN_DK
GDN_V = GDN_HEADS * GDN_DV
GDN_CONV_CH = 2 * GDN_QK + GDN_V
GMLP_W = GMLP_GROUPS * GMLP_DIM
D_MIX = RET_V + GDN_V + GMLP_W
D_IN = RET_QK * 2 + RET_V * 2 + GDN_CONV_CH + GDN_V + GDN_HEADS * 2 + GMLP_W * 2

LANES = 128
TILE = 128
INV_BLOCK = 16
MIXER_SEQS = 1

OFF_RQ = 0
OFF_RK = OFF_RQ + RET_QK
OFF_RV = OFF_RK + RET_QK
OFF_RG = OFF_RV + RET_V
OFF_GQKV = OFF_RG + RET_V
OFF_GZ = OFF_GQKV + GDN_CONV_CH
OFF_CU = OFF_GZ + GDN_V
OFF_CV = OFF_CU + GMLP_W
OFF_AB = OFF_CV + GMLP_W
D_PROJ = 6656

VMEM_LIMIT = 56 * 1024 * 1024


def _cparams(sem):
    return pltpu.CompilerParams(dimension_semantics=sem, vmem_limit_bytes=VMEM_LIMIT)


def _dot(a, b):
    return jnp.dot(a, b, preferred_element_type=F32)


def _dot_nt(a, b):
    return lax.dot_general(a, b, (((1,), (1,)), ((), ())), preferred_element_type=F32)


def _dot_tn(a, b):
    return lax.dot_general(a, b, (((0,), (0,)), ((), ())), preferred_element_type=F32)


def _silu(x):
    return x * jax.nn.sigmoid(x)


def _rms_rows(x, w):
    return x * lax.rsqrt(jnp.mean(x * x, axis=-1, keepdims=True) + EPS) * w


def _ffn_kernel(x_ref, nw_ref, wg_ref, wu_ref, wd_ref, *rest):
    fnw_ref = rest[0] if len(rest) == 3 else None
    o_ref, xn_ref = rest[-2:]
    j = pl.program_id(1)

    def delta():
        xn = xn_ref[...]
        g = _dot(xn, wg_ref[...])
        u = _dot(xn, wu_ref[...])
        a = (0.5 * _silu(g) * u).astype(BF16)
        return _dot(a, wd_ref[...])

    @pl.when(j == 0)
    def _():
        xn_ref[...] = _rms_rows(x_ref[...], nw_ref[...]).astype(BF16)
        o_ref[...] = x_ref[...] + delta()

    @pl.when(j != 0)
    def _():
        o_ref[...] += delta()

    if fnw_ref is not None:
        @pl.when(j == pl.num_programs(1) - 1)
        def _():
            o_ref[...] = _rms_rows(o_ref[...], fnw_ref[...])


def _ffn(x, nw, w_up, w_down, l, tm, tf, final_nw=None):
    n, d = x.shape
    dff = w_down.shape[1]
    nf = dff // tf
    vec = pl.BlockSpec((1, d), lambda i, j: (0, 0))
    return pl.pallas_call(
        _ffn_kernel,
        grid=(n // tm, nf),
        in_specs=[
            pl.BlockSpec((tm, d), lambda i, j: (i, 0)),
            vec,
            pl.BlockSpec((None, d, tf), lambda i, j: (l, 0, j)),
            pl.BlockSpec((None, d, tf), lambda i, j: (l, 0, j + nf)),
            pl.BlockSpec((None, tf, d), lambda i, j: (l, j, 0)),
        ] + ([] if final_nw is None else [vec]),
        out_specs=pl.BlockSpec((tm, d), lambda i, j: (i, 0)),
        out_shape=jax.ShapeDtypeStruct((n, d), F32),
        scratch_shapes=[pltpu.VMEM((tm, d), BF16)],
        compiler_params=_cparams(("parallel", "arbitrary")),
        name="ffn",
    )(x, nw, w_up, w_up, w_down, *(() if final_nw is None else (final_nw,)))


def _inproj_kernel(x_ref, nw_ref, w_ref, o_ref, xn_ref):
    @pl.when(pl.program_id(1) == 0)
    def _():
        xn_ref[...] = _rms_rows(x_ref[...], nw_ref[...]).astype(BF16)

    o_ref[...] = _dot(xn_ref[...], w_ref[...])


def _inproj(x, nw, w, l, tm, tn):
    n, d = x.shape
    dp = w.shape[2]
    return pl.pallas_call(
        _inproj_kernel,
        grid=(n // tm, dp // tn),
        in_specs=[
            pl.BlockSpec((tm, d), lambda i, j: (i, 0)),
            pl.BlockSpec((1, d), lambda i, j: (0, 0)),
            pl.BlockSpec((None, d, tn), lambda i, j: (l, 0, j)),
        ],
        out_specs=pl.BlockSpec((tm, tn), lambda i, j: (i, j)),
        out_shape=jax.ShapeDtypeStruct((n, dp), F32),
        scratch_shapes=[pltpu.VMEM((tm, d), BF16)],
        compiler_params=_cparams(("parallel", "arbitrary")),
        name="inproj",
    )(x, nw, w)


def _outproj_kernel(x_ref, c_ref, w_ref, o_ref):
    o_ref[...] = x_ref[...] + _dot(c_ref[...], w_ref[...])


def _outproj(x, cat, w, l, tm):
    n, d = x.shape
    k = cat.shape[1]
    return pl.pallas_call(
        _outproj_kernel,
        grid=(n // tm,),
        in_specs=[
            pl.BlockSpec((tm, d), lambda i: (i, 0)),
            pl.BlockSpec((tm, k), lambda i: (i, 0)),
            pl.BlockSpec((None, k, d), lambda i: (l, 0, 0)),
        ],
        out_specs=pl.BlockSpec((tm, d), lambda i: (i, 0)),
        out_shape=jax.ShapeDtypeStruct((n, d), F32),
        compiler_params=_cparams(("parallel",)),
        name="outproj",
    )(x, cat, w)


def _split3(x):
    hi = x.astype(BF16)
    r1 = x - hi.astype(F32)
    mid = r1.astype(BF16)
    lo = (r1 - mid.astype(F32)).astype(BF16)
    return hi, mid, lo


def _mixer_kernel(valid, emit_vn, ns,
                  proj_ref, cos_ref, sin_ref, dmat_ref, qdec_ref, kdec_ref, gam_ref,
                  alog_ref, dtb_ref, convw_ref, rgn_ref, gnw_ref, mnw_ref, ws_ref, bs_ref,
                  rs0_ref, gs0_ref, cv0_ref,
                  cat_ref, rs_ref, gs_ref, cvn_ref, *rest):
    if emit_vn:
        vn_ref, sret, sgdn, xbuf, gq_s, gk_s, gv_s, uv_s, a1_s, a2_s = rest
    else:
        vn_ref = None
        sret, sgdn, xbuf, gq_s, gk_s, gv_s, uv_s, a1_s, a2_s = rest
    tt = TILE
    t = pl.program_id(1)
    nt = pl.num_programs(1)
    seqs = range(ns)
    heads = range(GDN_HEADS)

    @pl.when(t == 0)
    def _():
        for s in seqs:
            for h in range(RET_HEADS):
                r = (h % 2) * RET_DK
                sret[s, h] = jnp.zeros((2 * RET_DK, RET_DV), F32)
                sret[s, h, r:r + RET_DK, :] = rs0_ref[s, h]
            for h in heads:
                sgdn[s, h] = gs0_ref[s, h]
            xbuf[s, 5:8, :] = cv0_ref[s]

    lane = lax.broadcasted_iota(jnp.int32, (tt, LANES), 1)
    ri = lax.broadcasted_iota(jnp.int32, (tt, tt), 0)
    ci = lax.broadcasted_iota(jnp.int32, (tt, tt), 1)
    tri = ri >= ci
    strict = ri > ci
    eye = jnp.where(ri == ci, 1.0, 0.0).astype(F32)

    first_half = (lane % RET_DK) < (RET_DK // 2)

    def rope(x):
        rot = jnp.where(first_half, pltpu.roll(x, LANES - RET_DK // 2, 1), pltpu.roll(x, RET_DK // 2, 1))
        return x * cos_ref[...] + rot * sin_ref[...]

    def retention_pair(s, j):
        qp = rope(proj_ref[s, :, OFF_RQ + LANES * j:OFF_RQ + LANES * (j + 1)])
        kp = rope(proj_ref[s, :, OFF_RK + LANES * j:OFF_RK + LANES * (j + 1)]) * (RET_DK ** -0.5)
        qpb = qp.astype(BF16)
        for e in range(2):
            h = 2 * j + e
            km = jnp.where(lane >= RET_DK if e else lane < RET_DK, kp, 0.0)
            v = proj_ref[s, :, OFF_RV + RET_DV * h:OFF_RV + RET_DV * (h + 1)].astype(BF16)
            s_prev = sret[s, h]
            scores = _dot_nt(qpb, km.astype(BF16)) * dmat_ref[h]
            lhs = jnp.concatenate([scores.astype(BF16), (qp * qdec_ref[h]).astype(BF16)], axis=1)
            o = _dot(lhs, jnp.concatenate([v, s_prev.astype(BF16)], axis=0))
            sret[s, h] = gam_ref[h] * s_prev + _dot_tn((km * kdec_ref[h]).astype(BF16), v)
            mu = jnp.mean(o, axis=-1, keepdims=True)
            oc = o - mu
            var = jnp.mean(oc * oc, axis=-1, keepdims=True)
            y = oc * lax.rsqrt(var + EPS) * rgn_ref[:, RET_DV * h:RET_DV * (h + 1)]
            g = proj_ref[s, :, OFF_RG + RET_DV * h:OFF_RG + RET_DV * (h + 1)]
            cat_ref[s, :, RET_DV * h:RET_DV * (h + 1)] = (y * _silu(g)).astype(BF16)

    def gmlp_group(s, g):
        cs = slice(GMLP_DIM * g, GMLP_DIM * (g + 1))
        v = proj_ref[s, :, OFF_CV + GMLP_DIM * g:OFF_CV + GMLP_DIM * (g + 1)]
        mu = jnp.mean(v, axis=-1, keepdims=True)
        vc = v - mu
        var = jnp.mean(vc * vc, axis=-1, keepdims=True)
        vg = vc * lax.rsqrt(var + EPS) * mnw_ref[:, cs]
        if emit_vn:
            vn_ref[s, :, cs] = vg
        wtri = jnp.where(tri, ws_ref[g], 0.0).astype(BF16)
        z = _dot(wtri, vg.astype(BF16)) + bs_ref[g]
        u = proj_ref[s, :, OFF_CU + GMLP_DIM * g:OFF_CU + GMLP_DIM * (g + 1)]
        cat_ref[s, :, RET_V + GDN_V + GMLP_DIM * g:RET_V + GDN_V + GMLP_DIM * (g + 1)] = (u * z).astype(BF16)

    independent = ([functools.partial(retention_pair, s, j) for j in range(RET_HEADS // 2) for s in seqs]
                   + [functools.partial(gmlp_group, s, g) for g in range(GMLP_GROUPS) for s in seqs])

    def fill():
        for _ in seqs:
            if independent:
                independent.pop(0)()

    for s in seqs:
        xbuf[s, 8:8 + tt, :] = proj_ref[s, :, OFF_GQKV:OFF_GQKV + GDN_CONV_CH]
        for p in range(GDN_CONV_CH // LANES):
            cs = slice(LANES * p, LANES * (p + 1))
            y = xbuf[s, 5:5 + tt, cs] * convw_ref[0:1, cs]
            for i in range(1, CONV_WIDTH):
                y = y + xbuf[s, 5 + i:5 + i + tt, cs] * convw_ref[i:i + 1, cs]
            y = _silu(y)
            kind, h = divmod(p, GDN_HEADS)
            if kind == 2:
                gv_s[s, h] = y
            else:
                yn = y * lax.rsqrt(jnp.sum(y * y, axis=-1, keepdims=True) + EPS)
                if kind == 0:
                    gq_s[s, h] = yn * (GDN_DK ** -0.5)
                else:
                    gk_s[s, h] = yn
        tail = xbuf[s, 5 + valid:8 + valid, :]
        xbuf[s, 5:8, :] = tail
        cvn_ref[s] = tail

    cum_mat = jnp.where(tri, 1.0, 0.0).astype(BF16)
    gcums, gcum_ts, betas = [], [], []
    for s in seqs:
        ab = proj_ref[s, :, OFF_AB:OFF_AB + LANES]
        log_a = -jnp.exp(alog_ref[...]) * jax.nn.softplus(ab + dtb_ref[...])
        beta = jax.nn.sigmoid(ab)
        if valid < tt:
            real = lax.broadcasted_iota(jnp.int32, (tt, LANES), 0) < valid
            log_a = jnp.where(real, log_a, 0.0)
            beta = jnp.where(real, beta, 0.0)
        la_hi, la_mid, la_lo = _split3(log_a)
        gcum = (_dot(cum_mat, la_hi) + _dot(cum_mat, la_mid)) + _dot(cum_mat, la_lo)
        gcums.append(gcum)
        gcum_ts.append(gcum.T)
        betas.append(beta)

    ls = {}
    for s in seqs:
        for h in heads:
            gcol = gcums[s][:, h:h + 1]
            grow = gcum_ts[s][h:h + 1, :]
            decay = jnp.exp(jnp.where(tri, gcol - grow, -1e30))
            bcol = betas[s][:, GDN_HEADS + h:GDN_HEADS + h + 1]
            kb = gk_s[s, h].astype(BF16)
            r = _dot_nt(jnp.concatenate([gq_s[s, h].astype(BF16), kb], axis=0), kb)
            a2_s[s, h, 0:tt, :] = (r[:tt] * decay).astype(BF16)
            ls[s, h] = bcol * r[tt:] * jnp.where(strict, decay, 0.0)

    zero_b = jnp.zeros((tt, tt), BF16)

    def pair_dot(a2, b2):
        bb = b2.astype(BF16)
        rhs = jnp.concatenate([jnp.concatenate([bb[:, :tt], zero_b], axis=1),
                               jnp.concatenate([zero_b, bb[:, tt:]], axis=1)], axis=0)
        return _dot(a2.astype(BF16), rhs)

    def same_block(b):
        return (ri // b) == (ci // b)

    chains = [(s, j) for s in seqs for j in range(GDN_HEADS // 2)]
    eye2 = jnp.concatenate([eye, eye], axis=1)
    l2 = {c: jnp.concatenate([ls[c[0], 2 * c[1]], ls[c[0], 2 * c[1] + 1]], axis=1) for c in chains}
    diag = same_block(INV_BLOCK)
    diag2 = jnp.concatenate([diag, diag], axis=1)
    pw = {c: jnp.where(diag2, -l2[c], 0.0) for c in chains}
    td = {c: eye2 + pw[c] for c in chains}
    n_fac = max(int(math.ceil(math.log2(min(INV_BLOCK, valid)))), 1)
    for it in range(n_fac):
        if it == 0:
            if n_fac > 1:
                pw = {c: pair_dot(pw[c], pw[c]) for c in chains}
        elif it == n_fac - 1:
            td = {c: td[c] + pair_dot(td[c], pw[c]) for c in chains}
        else:
            r = {c: pair_dot(jnp.concatenate([pw[c], td[c]], axis=0), pw[c]) for c in chains}
            pw = {c: r[c][:tt] for c in chains}
            td = {c: td[c] + r[c][tt:] for c in chains}
        fill()
    blk = INV_BLOCK
    while blk < valid:
        off = same_block(2 * blk) & jnp.logical_not(same_block(blk))
        off2 = jnp.concatenate([off, off], axis=1)
        y = {c: pair_dot(jnp.where(off2, l2[c], 0.0), td[c]) for c in chains}
        fill()
        z = {c: pair_dot(td[c], y[c]) for c in chains}
        td = {c: td[c] - z[c] for c in chains}
        blk *= 2

    gls = {}
    for s in seqs:
        for h in heads:
            j, e = divmod(h, 2)
            tinv = td[s, j][:, tt * e:tt * (e + 1)]
            gcol = gcums[s][:, h:h + 1]
            eg = jnp.exp(gcol)
            bcol = betas[s][:, GDN_HEADS + h:GDN_HEADS + h + 1]
            k = gk_s[s, h]
            rhs = jnp.concatenate([bcol * gv_s[s, h], (bcol * eg) * k], axis=1).astype(BF16)
            sol = _dot(tinv.astype(BF16), rhs)
            uv_s[s, h] = sol[:, :GDN_DV]
            a1_s[s, h, 0:tt, :] = sol[:, GDN_DV:].astype(BF16)
            a1_s[s, h, tt:2 * tt, :] = (gq_s[s, h] * eg).astype(BF16)
            glast = gcums[s][tt - 1:tt, h:h + 1]
            a2_s[s, h, tt:2 * tt, :] = (k * jnp.exp(glast - gcol)).T.astype(BF16)
            gls[s, h] = jnp.exp(glast)

    sh = [(s, h) for s in seqs for h in heads]
    ss = {c: sgdn[c] for c in sh}
    r1 = {c: _dot(a1_s[c], ss[c].astype(BF16)) for c in sh}
    ub = {c: (uv_s[c] - r1[c][:tt]).astype(BF16) for c in sh}
    r2 = {c: _dot(a2_s[c], ub[c]) for c in sh}
    for s, h in sh:
        sgdn[s, h] = ss[s, h] * gls[s, h] + r2[s, h][tt:]
        o = r1[s, h][tt:] + r2[s, h][:tt]
        y = o * lax.rsqrt(jnp.mean(o * o, axis=-1, keepdims=True) + EPS) * gnw_ref[...]
        z = proj_ref[s, :, OFF_GZ + GDN_DV * h:OFF_GZ + GDN_DV * (h + 1)]
        cat_ref[s, :, RET_V + GDN_DV * h:RET_V + GDN_DV * (h + 1)] = (y * _silu(z)).astype(BF16)

    while independent:
        fill()

    @pl.when(t == nt - 1)
    def _():
        for s in seqs:
            for h in range(RET_HEADS):
                r = (h % 2) * RET_DK
                rs_ref[s, h] = sret[s, h, r:r + RET_DK, :]
            for h in heads:
                gs_ref[s, h] = sgdn[s, h]


def _mixer(proj, cosf, sinf, consts, lp, rs0, gs0, cv0, valid, emit_vn):
    b, t, dp = proj.shape
    tt = TILE
    ns = MIXER_SEQS if b % MIXER_SEQS == 0 else 1
    dmat, qdec, kdec, gam = consts
    full = lambda a: pl.BlockSpec(a.shape, lambda i, j, _n=a.ndim: (0,) * _n)
    in_arrays = [proj, cosf, sinf, dmat, qdec, kdec, gam,
                 lp["alog"], lp["dtb"], lp["convw"], lp["rgn"], lp["gnw"], lp["mnw"], lp["ws"], lp["bs"],
                 rs0, gs0, cv0]
    state_specs = [
        pl.BlockSpec((ns, RET_HEADS, RET_DK, RET_DV), lambda i, j: (i, 0, 0, 0)),
        pl.BlockSpec((ns, GDN_HEADS, GDN_DK, GDN_DV), lambda i, j: (i, 0, 0, 0)),
        pl.BlockSpec((ns, CONV_WIDTH - 1, GDN_CONV_CH), lambda i, j: (i, 0, 0)),
    ]
    in_specs = [
        pl.BlockSpec((ns, tt, dp), lambda i, j: (i, j, 0)),
        pl.BlockSpec((tt, LANES), lambda i, j: (j, 0)),
        pl.BlockSpec((tt, LANES), lambda i, j: (j, 0)),
    ] + [full(a) for a in in_arrays[3:15]] + state_specs
    out_shape = [
        jax.ShapeDtypeStruct((b, t, D_MIX), BF16),
        jax.ShapeDtypeStruct((b, RET_HEADS, RET_DK, RET_DV), F32),
        jax.ShapeDtypeStruct((b, GDN_HEADS, GDN_DK, GDN_DV), F32),
        jax.ShapeDtypeStruct((b, CONV_WIDTH - 1, GDN_CONV_CH), F32),
    ]
    out_specs = [pl.BlockSpec((ns, tt, D_MIX), lambda i, j: (i, j, 0))] + state_specs
    if emit_vn:
        out_shape.append(jax.ShapeDtypeStruct((b, t, GMLP_W), F32))
        out_specs.append(pl.BlockSpec((ns, tt, GMLP_W), lambda i, j: (i, j, 0)))
    hs = (ns, GDN_HEADS, tt, LANES)
    scratch = [
        pltpu.VMEM((ns, RET_HEADS, 2 * RET_DK, RET_DV), F32),
        pltpu.VMEM((ns, GDN_HEADS, GDN_DK, GDN_DV), F32),
        pltpu.VMEM((ns, tt + 8, GDN_CONV_CH), F32),
        pltpu.VMEM(hs, F32), pltpu.VMEM(hs, F32), pltpu.VMEM(hs, F32),
        pltpu.VMEM(hs, F32),
        pltpu.VMEM((ns, GDN_HEADS, 2 * tt, LANES), BF16),
        pltpu.VMEM((ns, GDN_HEADS, 2 * tt, LANES), BF16),
    ]
    return pl.pallas_call(
        functools.partial(_mixer_kernel, valid, emit_vn, ns),
        grid=(b // ns, t // tt),
        in_specs=in_specs,
        out_specs=out_specs,
        out_shape=out_shape,
        scratch_shapes=scratch,
        compiler_params=_cparams(("parallel", "arbitrary")),
        name="mixer",
    )(*in_arrays)


def _pad_rows(a, rows, axis):
    pad = [(0, 0)] * a.ndim
    pad[axis] = (0, rows - a.shape[axis])
    return jnp.pad(a, pad)


def _rope_tables(pos):
    half = RET_DK // 2
    inv = jnp.power(ROPE_BASE, -jnp.arange(half, dtype=F32) / half)
    ang = pos.astype(F32)[:, None] * inv[None, :]
    cos, sin = jnp.cos(ang), jnp.sin(ang)
    reps = LANES // RET_DK
    return jnp.tile(jnp.concatenate([cos, cos], axis=1), (1, reps)), jnp.tile(jnp.concatenate([-sin, sin], axis=1), (1, reps))


def _ret_consts(cr):
    log_gamma = jnp.log1p(-jnp.power(2.0, -5.0 - jnp.arange(RET_HEADS, dtype=F32)))
    idx = jnp.arange(cr, dtype=F32)
    diff = idx[:, None] - idx[None, :]
    causal = diff >= 0
    dmat = jnp.where(causal[None], jnp.exp(jnp.where(causal, diff, 0.0)[None] * log_gamma[:, None, None]), 0.0)
    kdec = jnp.exp(log_gamma[:, None] * (cr - 1 - idx)[None, :])[:, :, None]
    qdec = jnp.exp(log_gamma[:, None] * (idx + 1.0)[None, :])[:, :, None]
    gam = jnp.broadcast_to(jnp.exp(cr * log_gamma)[:, None, None], (RET_HEADS, 1, LANES))
    dmat = _pad_rows(_pad_rows(dmat, TILE, 1), TILE, 2)
    return dmat, _pad_rows(qdec, TILE, 1), _pad_rows(kdec, TILE, 1), gam


def _pad_lanes(v):
    return jnp.pad(v, (0, LANES - v.shape[0]))[None, :]


def _tiles(n):
    return dict(tm_ffn=min(1024, n), tf=512, tm_in=min(512, n), tn_in=D_PROJ // 2, tm_out=min(512, n))


def _trunk(x, pos, rs0, gs0, cv0, p, emit_vn):
    b, t, d = x.shape
    n = b * t
    tl = _tiles(n)
    valid = min(t, TILE)
    tpad = -(-t // TILE) * TILE
    cosf, sinf = _rope_tables(pos)
    cosf, sinf = _pad_rows(cosf, tpad, 0), _pad_rows(sinf, tpad, 0)
    consts = _ret_consts(valid)
    x = x.reshape(n, d)
    outs = []
    for l in range(DEPTH):
        x = _ffn(x, p["norm_ffn1_w"][l][None], p["ffn1_w_up"], p["ffn1_w_down"], l, tl["tm_ffn"], tl["tf"])
        proj = _inproj(x, p["norm_mix_w"][l][None], p["w_in"], l, tl["tm_in"], tl["tn_in"])
        lp = {
            "alog": _pad_lanes(p["gdn_a_log"][l]), "dtb": _pad_lanes(p["gdn_dt_bias"][l]),
            "convw": p["gdn_conv_w"][l], "rgn": p["ret_gn_w"][l][None], "gnw": p["gdn_norm_w"][l][None],
            "mnw": p["gmlp_norm_w"][l][None],
            "ws": _pad_rows(_pad_rows(p["gmlp_ws"][l][:, :valid, :valid], TILE, 1), TILE, 2),
            "bs": _pad_rows(p["gmlp_bs"][l][:, :valid, None], TILE, 1),
        }
        res = _mixer(_pad_rows(proj.reshape(b, t, -1), tpad, 1), cosf, sinf, consts, lp,
                     rs0[l], gs0[l], cv0[l], valid, emit_vn)
        cat = res[0][:, :t].reshape(n, D_MIX)
        x = _outproj(x, cat, p["w_out"], l, tl["tm_out"])
        x = _ffn(x, p["norm_ffn2_w"][l][None], p["ffn2_w_up"], p["ffn2_w_down"], l, tl["tm_ffn"], tl["tf"],
                 final_nw=p["final_norm_w"][None] if l == DEPTH - 1 else None)
        outs.append(tuple(res[1:4]) + ((res[4][:, :t],) if emit_vn else ()))
    y = x.reshape(b, t, d)
    return (y,) + tuple(jnp.stack([o[i] for o in outs]) for i in range(len(outs[0])))


def _prep_params(p):
    q = dict(p)
    for k in ("ffn1_w_up", "ffn1_w_down", "ffn2_w_up", "ffn2_w_down", "w_out"):
        q[k] = p[k].astype(BF16)
    w = p["w_in"]
    ab0 = 2 * RET_QK + 2 * RET_V + GDN_CONV_CH + GDN_V
    wb = w.astype(BF16)
    q["w_in"] = jnp.concatenate(
        [wb[:, :, :ab0], wb[:, :, ab0 + 2 * GDN_HEADS:], wb[:, :, ab0:ab0 + 2 * GDN_HEADS],
         jnp.zeros(w.shape[:2] + (D_PROJ - D_IN,), BF16)], axis=-1)
    return q


def kernel(x_prompt, x_sample, state_ret, state_gdn, state_conv, norm_ffn1_w, ffn1_w_up, ffn1_w_down,
           norm_mix_w, w_in, ret_gn_w, gdn_conv_w, gdn_a_log, gdn_dt_bias, gdn_norm_w, gmlp_ws, gmlp_bs,
           gmlp_norm_w, w_out, norm_ffn2_w, ffn2_w_up, ffn2_w_down, final_norm_w):
    p = _prep_params({
        "norm_ffn1_w": norm_ffn1_w, "ffn1_w_up": ffn1_w_up, "ffn1_w_down": ffn1_w_down,
        "norm_mix_w": norm_mix_w, "w_in": w_in, "ret_gn_w": ret_gn_w, "gdn_conv_w": gdn_conv_w,
        "gdn_a_log": gdn_a_log, "gdn_dt_bias": gdn_dt_bias, "gdn_norm_w": gdn_norm_w,
        "gmlp_ws": gmlp_ws, "gmlp_bs": gmlp_bs, "gmlp_norm_w": gmlp_norm_w, "w_out": w_out,
        "norm_ffn2_w": norm_ffn2_w, "ffn2_w_up": ffn2_w_up, "ffn2_w_down": ffn2_w_down,
        "final_norm_w": final_norm_w})
    bp, tp = x_prompt.shape[0], x_prompt.shape[1]
    bs, ts = x_sample.shape[0], x_sample.shape[1]
    ret0 = jnp.zeros((DEPTH, bp, RET_HEADS, RET_DK, RET_DV), F32)
    gdn0 = jnp.zeros((DEPTH, bp, GDN_HEADS, GDN_DK, GDN_DV), F32)
    conv0 = jnp.zeros((DEPTH, bp, CONV_WIDTH - 1, GDN_CONV_CH), F32)
    pos_p = jnp.arange(tp, dtype=jnp.int32)
    y_p, ret_p, gdn_p, conv_p = _trunk(x_prompt, pos_p, ret0, gdn0, conv0, p, emit_vn=False)
    pos_s = PAST_LEN + jnp.arange(ts, dtype=jnp.int32)
    y_s, ret_s, gdn_s, conv_s, vn_s = _trunk(x_sample, pos_s, state_ret, state_gdn, state_conv, p, emit_vn=True)
    return (y_p, y_s, ret_p, ret_s, gdn_p, gdn_s, conv_p, conv_s, vn_s)
```

```python
import functools
import math

import jax
import jax.numpy as jnp
from jax import lax
from jax.experimental import pallas as pl
from jax.experimental.pallas import tpu as pltpu

F32 = jnp.float32
BF16 = jnp.bfloat16

D_MODEL = 2048
DEPTH = 4
PAST_LEN = 4096
EPS = 1e-6
ROPE_BASE = 10000.0
RET_HEADS = 6
RET_DK = 64
RET_DV = 128
GDN_HEADS = 6
GDN_DK = 128
GDN_DV = 128
CONV_WIDTH = 4
GMLP_GROUPS = 4
GMLP_DIM = 128
GMLP_CHUNK = 128
D_FF = 5632

RET_QK = RET_HEADS * RET_DK
RET_V = RET_HEADS * RET_DV
GDN_QK = GDN_HEADS * GDN_DK
GDN_V = GDN_HEADS * GDN_DV
GDN_CONV_CH = 2 * GDN_QK + GDN_V
GMLP_W = GMLP_GROUPS * GMLP_DIM
D_MIX = RET_V + GDN_V + GMLP_W
D_IN = RET_QK * 2 + RET_V * 2 + GDN_CONV_CH + GDN_V + GDN_HEADS * 2 + GMLP_W * 2

LANES = 128
TILE = 128
INV_BLOCK = 16
PROJ_COLS = 256

OFF_RQ = 0
OFF_RK = OFF_RQ + RET_QK
OFF_RV = OFF_RK + RET_QK
OFF_RG = OFF_RV + RET_V
OFF_GQKV = OFF_RG + RET_V
OFF_GZ = OFF_GQKV + GDN_CONV_CH
OFF_CU = OFF_GZ + GDN_V
OFF_CV = OFF_CU + GMLP_W
OFF_AB = OFF_CV + GMLP_W
D_PROJ = 6656

VMEM_LIMIT = 56 * 1024 * 1024


def _cparams(sem):
    return pltpu.CompilerParams(dimension_semantics=sem, vmem_limit_bytes=VMEM_LIMIT)


def _dot(a, b):
    return jnp.dot(a, b, preferred_element_type=F32)


def _dot_nt(a, b):
    return lax.dot_general(a, b, (((1,), (1,)), ((), ())), preferred_element_type=F32)


def _dot_tn(a, b):
    return lax.dot_general(a, b, (((0,), (0,)), ((), ())), preferred_element_type=F32)


def _silu(x):
    return x * jax.nn.sigmoid(x)


def _rms_rows(x, w):
    return x * lax.rsqrt(jnp.mean(x * x, axis=-1, keepdims=True) + EPS) * w


def _ffn_kernel(x_ref, nw_ref, wg_ref, wu_ref, wd_ref, *rest):
    fnw_ref = rest[0] if len(rest) == 3 else None
    o_ref, xn_ref = rest[-2:]
    j = pl.program_id(1)

    def delta():
        xn = xn_ref[...]
        g = _dot(xn, wg_ref[...])
        u = _dot(xn, wu_ref[...])
        a = (0.5 * _silu(g) * u).astype(BF16)
        return _dot(a, wd_ref[...])

    @pl.when(j == 0)
    def _():
        xn_ref[...] = _rms_rows(x_ref[...], nw_ref[...]).astype(BF16)
        o_ref[...] = x_ref[...] + delta()

    @pl.when(j != 0)
    def _():
        o_ref[...] += delta()

    if fnw_ref is not None:
        @pl.when(j == pl.num_programs(1) - 1)
        def _():
            o_ref[...] = _rms_rows(o_ref[...], fnw_ref[...])


def _ffn(x, nw, w_up, w_down, l, tm, tf, final_nw=None):
    n, d = x.shape
    dff = w_down.shape[1]
    nf = dff // tf
    vec = pl.BlockSpec((1, d), lambda i, j: (0, 0))
    return pl.pallas_call(
        _ffn_kernel,
        grid=(n // tm, nf),
        in_specs=[
            pl.BlockSpec((tm, d), lambda i, j: (i, 0)),
            vec,
            pl.BlockSpec((None, d, tf), lambda i, j: (l, 0, j)),
            pl.BlockSpec((None, d, tf), lambda i, j: (l, 0, j + nf)),
            pl.BlockSpec((None, tf, d), lambda i, j: (l, j, 0)),
        ] + ([] if final_nw is None else [vec]),
        out_specs=pl.BlockSpec((tm, d), lambda i, j: (i, 0)),
        out_shape=jax.ShapeDtypeStruct((n, d), F32),
        scratch_shapes=[pltpu.VMEM((tm, d), BF16)],
        compiler_params=_cparams(("parallel", "arbitrary")),
        name="ffn",
    )(x, nw, w_up, w_up, w_down, *(() if final_nw is None else (final_nw,)))


def _outproj_kernel(x_ref, c_ref, w_ref, o_ref):
    o_ref[...] = x_ref[...] + _dot(c_ref[...], w_ref[...])


def _outproj(x, cat, w, l, tm):
    n, d = x.shape
    k = cat.shape[1]
    return pl.pallas_call(
        _outproj_kernel,
        grid=(n // tm,),
        in_specs=[
            pl.BlockSpec((tm, d), lambda i: (i, 0)),
            pl.BlockSpec((tm, k), lambda i: (i, 0)),
            pl.BlockSpec((None, k, d), lambda i: (l, 0, 0)),
        ],
        out_specs=pl.BlockSpec((tm, d), lambda i: (i, 0)),
        out_shape=jax.ShapeDtypeStruct((n, d), F32),
        compiler_params=_cparams(("parallel",)),
        name="outproj",
    )(x, cat, w)


def _split3(x):
    hi = x.astype(BF16)
    r1 = x - hi.astype(F32)
    mid = r1.astype(BF16)
    lo = (r1 - mid.astype(F32)).astype(BF16)
    return hi, mid, lo


def _mixer_kernel(valid, emit_vn, pipelined,
                  x0_ref, x1_ref, nw_ref, win_ref,
                  cos_ref, sin_ref, dmat_ref, qdec_ref, kdec_ref, gam_ref,
                  alog_ref, dtb_ref, convw_ref, rgn_ref, gnw_ref, mnw_ref, ws_ref, bs_ref,
                  rs0_ref, gs0_ref, cv0_ref,
                  cat_ref, rs_ref, gs_ref, cvn_ref, *rest):
    if emit_vn:
        vn_ref, proj_s, xn_s, sret, sgdn, xbuf, gq_s, gk_s, gv_s, uv_s, a1_s, a2_s = rest
    else:
        vn_ref = None
        proj_s, xn_s, sret, sgdn, xbuf, gq_s, gk_s, gv_s, uv_s, a1_s, a2_s = rest
    tt = TILE
    t = pl.program_id(1)
    nt = pl.num_programs(1)
    cur = t % 2
    heads = range(GDN_HEADS)
    col_blocks = [slice(PROJ_COLS * c, PROJ_COLS * (c + 1)) for c in range(D_PROJ // PROJ_COLS)]

    @pl.when(t == 0)
    def _():
        for h in range(RET_HEADS):
            r = (h % 2) * RET_DK
            sret[h] = jnp.zeros((2 * RET_DK, RET_DV), F32)
            sret[h, r:r + RET_DK, :] = rs0_ref[0, h]
        for h in heads:
            sgdn[h] = gs0_ref[0, h]
        xbuf[5:8, :] = cv0_ref[0]
        xn0 = _rms_rows(x0_ref[0], nw_ref[...]).astype(BF16)
        for cs in col_blocks:
            proj_s[0, :, cs] = _dot(xn0, win_ref[:, cs])

    def proj(off, width):
        return proj_s[cur, :, off:off + width]

    lane = lax.broadcasted_iota(jnp.int32, (tt, LANES), 1)
    ri = lax.broadcasted_iota(jnp.int32, (tt, tt), 0)
    ci = lax.broadcasted_iota(jnp.int32, (tt, tt), 1)
    tri = ri >= ci
    strict = ri > ci
    eye = jnp.where(ri == ci, 1.0, 0.0).astype(F32)

    def project_next(cs):
        proj_s[1 - cur, :, cs] = _dot(xn_s[...], win_ref[:, cs])

    first_half = (lane % RET_DK) < (RET_DK // 2)

    def rope(x):
        rot = jnp.where(first_half, pltpu.roll(x, LANES - RET_DK // 2, 1), pltpu.roll(x, RET_DK // 2, 1))
        return x * cos_ref[...] + rot * sin_ref[...]

    def retention_pair(j):
        qp = rope(proj(OFF_RQ + LANES * j, LANES))
        kp = rope(proj(OFF_RK + LANES * j, LANES)) * (RET_DK ** -0.5)
        qpb = qp.astype(BF16)
        for e in range(2):
            h = 2 * j + e
            km = jnp.where(lane >= RET_DK if e else lane < RET_DK, kp, 0.0)
            v = proj(OFF_RV + RET_DV * h, RET_DV).astype(BF16)
            s_prev = sret[h]
            scores = _dot_nt(qpb, km.astype(BF16)) * dmat_ref[h]
            lhs = jnp.concatenate([scores.astype(BF16), (qp * qdec_ref[h]).astype(BF16)], axis=1)
            o = _dot(lhs, jnp.concatenate([v, s_prev.astype(BF16)], axis=0))
            sret[h] = gam_ref[h] * s_prev + _dot_tn((km * kdec_ref[h]).astype(BF16), v)
            mu = jnp.mean(o, axis=-1, keepdims=True)
            oc = o - mu
            var = jnp.mean(oc * oc, axis=-1, keepdims=True)
            y = oc * lax.rsqrt(var + EPS) * rgn_ref[:, RET_DV * h:RET_DV * (h + 1)]
            cat_ref[0, :, RET_DV * h:RET_DV * (h + 1)] = (y * _silu(proj(OFF_RG + RET_DV * h, RET_DV))).astype(BF16)

    def gmlp_group(g):
        cs = slice(GMLP_DIM * g, GMLP_DIM * (g + 1))
        v = proj(OFF_CV + GMLP_DIM * g, GMLP_DIM)
        mu = jnp.mean(v, axis=-1, keepdims=True)
        vc = v - mu
        var = jnp.mean(vc * vc, axis=-1, keepdims=True)
        vg = vc * lax.rsqrt(var + EPS) * mnw_ref[:, cs]
        if emit_vn:
            vn_ref[0, :, cs] = vg
        wtri = jnp.where(tri, ws_ref[g], 0.0).astype(BF16)
        z = _dot(wtri, vg.astype(BF16)) + bs_ref[g]
        u = proj(OFF_CU + GMLP_DIM * g, GMLP_DIM)
        cat_ref[0, :, RET_V + GDN_V + GMLP_DIM * g:RET_V + GDN_V + GMLP_DIM * (g + 1)] = (u * z).astype(BF16)

    side = [functools.partial(retention_pair, j) for j in range(RET_HEADS // 2)]
    side += [functools.partial(gmlp_group, g) for g in range(GMLP_GROUPS)]
    dense = [functools.partial(project_next, cs) for cs in col_blocks] if pipelined else []

    def fill(n_dense, n_side=0):
        for _ in range(n_dense):
            if dense:
                dense.pop(0)()
        for _ in range(n_side):
            if side:
                side.pop(0)()

    if pipelined:
        xn_s[...] = _rms_rows(x1_ref[0], nw_ref[...]).astype(BF16)

    xbuf[8:8 + tt, :] = proj(OFF_GQKV, GDN_CONV_CH)
    for p in range(GDN_CONV_CH // LANES):
        cs = slice(LANES * p, LANES * (p + 1))
        y = xbuf[5:5 + tt, cs] * convw_ref[0:1, cs]
        for i in range(1, CONV_WIDTH):
            y = y + xbuf[5 + i:5 + i + tt, cs] * convw_ref[i:i + 1, cs]
        y = _silu(y)
        kind, h = divmod(p, GDN_HEADS)
        if kind == 2:
            gv_s[h] = y
        else:
            yn = y * lax.rsqrt(jnp.sum(y * y, axis=-1, keepdims=True) + EPS)
            if kind == 0:
                gq_s[h] = yn * (GDN_DK ** -0.5)
            else:
                gk_s[h] = yn
        if p % 3 == 2:
            fill(1)
    tail = xbuf[5 + valid:8 + valid, :]
    xbuf[5:8, :] = tail
    cvn_ref[0] = tail

    ab = proj(OFF_AB, LANES)
    log_a = -jnp.exp(alog_ref[...]) * jax.nn.softplus(ab + dtb_ref[...])
    beta = jax.nn.sigmoid(ab)
    if valid < tt:
        real = lax.broadcasted_iota(jnp.int32, (tt, LANES), 0) < valid
        log_a = jnp.where(real, log_a, 0.0)
        beta = jnp.where(real, beta, 0.0)
    cum_mat = jnp.where(tri, 1.0, 0.0).astype(BF16)
    la_hi, la_mid, la_lo = _split3(log_a)
    gcum = (_dot(cum_mat, la_hi) + _dot(cum_mat, la_mid)) + _dot(cum_mat, la_lo)
    gcum_t = gcum.T

    ls = []
    for h in heads:
        gcol = gcum[:, h:h + 1]
        grow = gcum_t[h:h + 1, :]
        decay = jnp.exp(jnp.where(tri, gcol - grow, -1e30))
        bcol = beta[:, GDN_HEADS + h:GDN_HEADS + h + 1]
        kb = gk_s[h].astype(BF16)
        r = _dot_nt(jnp.concatenate([gq_s[h].astype(BF16), kb], axis=0), kb)
        a2_s[h, 0:tt, :] = (r[:tt] * decay).astype(BF16)
        ls.append(bcol * r[tt:] * jnp.where(strict, decay, 0.0))
        fill(1)

    zero_b = jnp.zeros((tt, tt), BF16)

    def pair_dot(a2, b2):
        bb = b2.astype(BF16)
        rhs = jnp.concatenate([jnp.concatenate([bb[:, :tt], zero_b], axis=1),
                               jnp.concatenate([zero_b, bb[:, tt:]], axis=1)], axis=0)
        return _dot(a2.astype(BF16), rhs)

    def same_block(b):
        return (ri // b) == (ci // b)

    pairs = range(GDN_HEADS // 2)
    eye2 = jnp.concatenate([eye, eye], axis=1)
    l2 = [jnp.concatenate([ls[2 * j], ls[2 * j + 1]], axis=1) for j in pairs]
    diag = same_block(INV_BLOCK)
    diag2 = jnp.concatenate([diag, diag], axis=1)
    pw = [jnp.where(diag2, -l2[j], 0.0) for j in pairs]
    td = [eye2 + pw[j] for j in pairs]
    n_fac = max(int(math.ceil(math.log2(min(INV_BLOCK, valid)))), 1)
    for it in range(n_fac):
        if it == 0:
            if n_fac > 1:
                pw = [pair_dot(pw[j], pw[j]) for j in pairs]
        elif it == n_fac - 1:
            td = [td[j] + pair_dot(td[j], pw[j]) for j in pairs]
        else:
            r = [pair_dot(jnp.concatenate([pw[j], td[j]], axis=0), pw[j]) for j in pairs]
            pw = [r[j][:tt] for j in pairs]
            td = [td[j] + r[j][tt:] for j in pairs]
        fill(1, 1)
    blk = INV_BLOCK
    while blk < valid:
        off = same_block(2 * blk) & jnp.logical_not(same_block(blk))
        off2 = jnp.concatenate([off, off], axis=1)
        y = [pair_dot(jnp.where(off2, l2[j], 0.0), td[j]) for j in pairs]
        fill(1, 1)
        z = [pair_dot(td[j], y[j]) for j in pairs]
        td = [td[j] - z[j] for j in pairs]
        fill(1)
        blk *= 2

    gls = []
    for h in heads:
        j, e = divmod(h, 2)
        tinv = td[j][:, tt * e:tt * (e + 1)]
        gcol = gcum[:, h:h + 1]
        eg = jnp.exp(gcol)
        bcol = beta[:, GDN_HEADS + h:GDN_HEADS + h + 1]
        k = gk_s[h]
        rhs = jnp.concatenate([bcol * gv_s[h], (bcol * eg) * k], axis=1).astype(BF16)
        sol = _dot(tinv.astype(BF16), rhs)
        uv_s[h] = sol[:, :GDN_DV]
        a1_s[h, 0:tt, :] = sol[:, GDN_DV:].astype(BF16)
        a1_s[h, tt:2 * tt, :] = (gq_s[h] * eg).astype(BF16)
        glast = gcum[tt - 1:tt, h:h + 1]
        a2_s[h, tt:2 * tt, :] = (k * jnp.exp(glast - gcol)).T.astype(BF16)
        gls.append(jnp.exp(glast))
    fill(2)

    ss = [sgdn[h] for h in heads]
    r1 = [_dot(a1_s[h], ss[h].astype(BF16)) for h in heads]
    fill(2)
    ub = [(uv_s[h] - r1[h][:tt]).astype(BF16) for h in heads]
    r2 = [_dot(a2_s[h], ub[h]) for h in heads]
    fill(2)
    for h in heads:
        sgdn[h] = ss[h] * gls[h] + r2[h][tt:]
        o = r1[h][tt:] + r2[h][:tt]
        y = o * lax.rsqrt(jnp.mean(o * o, axis=-1, keepdims=True) + EPS) * gnw_ref[...]
        cat_ref[0, :, RET_V + GDN_DV * h:RET_V + GDN_DV * (h + 1)] = (y * _silu(proj(OFF_GZ + GDN_DV * h, GDN_DV))).astype(BF16)

    fill(len(dense), len(side))

    @pl.when(t == nt - 1)
    def _():
        for h in range(RET_HEADS):
            r = (h % 2) * RET_DK
            rs_ref[0, h] = sret[h, r:r + RET_DK, :]
        for h in heads:
            gs_ref[0, h] = sgdn[h]


def _mixer(x, nw, w_in, l, cosf, sinf, consts, lp, rs0, gs0, cv0, valid, emit_vn):
    b, t, d = x.shape
    tt = TILE
    nt = t // tt
    dmat, qdec, kdec, gam = consts
    full = lambda a: pl.BlockSpec(a.shape, lambda i, j, _n=a.ndim: (0,) * _n)
    small = [cosf, sinf, dmat, qdec, kdec, gam,
             lp["alog"], lp["dtb"], lp["convw"], lp["rgn"], lp["gnw"], lp["mnw"], lp["ws"], lp["bs"]]
    state_specs = [
        pl.BlockSpec((1, RET_HEADS, RET_DK, RET_DV), lambda i, j: (i, 0, 0, 0)),
        pl.BlockSpec((1, GDN_HEADS, GDN_DK, GDN_DV), lambda i, j: (i, 0, 0, 0)),
        pl.BlockSpec((1, CONV_WIDTH - 1, GDN_CONV_CH), lambda i, j: (i, 0, 0)),
    ]
    in_specs = [
        pl.BlockSpec((1, tt, d), lambda i, j: (i, 0, 0)),
        pl.BlockSpec((1, tt, d), lambda i, j: (i, jnp.minimum(j + 1, nt - 1), 0)),
        pl.BlockSpec((1, d), lambda i, j: (0, 0)),
        pl.BlockSpec((None, d, D_PROJ), lambda i, j: (l, 0, 0), pipeline_mode=pl.Buffered(1)),
        pl.BlockSpec((tt, LANES), lambda i, j: (j, 0)),
        pl.BlockSpec((tt, LANES), lambda i, j: (j, 0)),
    ] + [full(a) for a in small[2:]] + state_specs
    out_shape = [
        jax.ShapeDtypeStruct((b, t, D_MIX), BF16),
        jax.ShapeDtypeStruct((b, RET_HEADS, RET_DK, RET_DV), F32),
        jax.ShapeDtypeStruct((b, GDN_HEADS, GDN_DK, GDN_DV), F32),
        jax.ShapeDtypeStruct((b, CONV_WIDTH - 1, GDN_CONV_CH), F32),
    ]
    out_specs = [pl.BlockSpec((1, tt, D_MIX), lambda i, j: (i, j, 0))] + state_specs
    if emit_vn:
        out_shape.append(jax.ShapeDtypeStruct((b, t, GMLP_W), F32))
        out_specs.append(pl.BlockSpec((1, tt, GMLP_W), lambda i, j: (i, j, 0)))
    hs = (GDN_HEADS, tt, LANES)
    scratch = [
        pltpu.VMEM((2, tt, D_PROJ), F32),
        pltpu.VMEM((tt, d), BF16),
        pltpu.VMEM((RET_HEADS, 2 * RET_DK, RET_DV), F32),
        pltpu.VMEM((GDN_HEADS, GDN_DK, GDN_DV), F32),
        pltpu.VMEM((tt + 8, GDN_CONV_CH), F32),
        pltpu.VMEM(hs, F32), pltpu.VMEM(hs, F32), pltpu.VMEM(hs, F32),
        pltpu.VMEM(hs, F32),
        pltpu.VMEM((GDN_HEADS, 2 * tt, LANES), BF16),
        pltpu.VMEM((GDN_HEADS, 2 * tt, LANES), BF16),
    ]
    return pl.pallas_call(
        functools.partial(_mixer_kernel, valid, emit_vn, nt > 1),
        grid=(b, nt),
        in_specs=in_specs,
        out_specs=out_specs,
        out_shape=out_shape,
        scratch_shapes=scratch,
        compiler_params=_cparams(("parallel", "arbitrary")),
        name="mixer",
    )(x, x, nw, w_in, *small, rs0, gs0, cv0)


def _pad_rows(a, rows, axis):
    pad = [(0, 0)] * a.ndim
    pad[axis] = (0, rows - a.shape[axis])
    return jnp.pad(a, pad)


def _rope_tables(pos):
    half = RET_DK // 2
    inv = jnp.power(ROPE_BASE, -jnp.arange(half, dtype=F32) / half)
    ang = pos.astype(F32)[:, None] * inv[None, :]
    cos, sin = jnp.cos(ang), jnp.sin(ang)
    reps = LANES // RET_DK
    return jnp.tile(jnp.concatenate([cos, cos], axis=1), (1, reps)), jnp.tile(jnp.concatenate([-sin, sin], axis=1), (1, reps))


def _ret_consts(cr):
    log_gamma = jnp.log1p(-jnp.power(2.0, -5.0 - jnp.arange(RET_HEADS, dtype=F32)))
    idx = jnp.arange(cr, dtype=F32)
    diff = idx[:, None] - idx[None, :]
    causal = diff >= 0
    dmat = jnp.where(causal[None], jnp.exp(jnp.where(causal, diff, 0.0)[None] * log_gamma[:, None, None]), 0.0)
    kdec = jnp.exp(log_gamma[:, None] * (cr - 1 - idx)[None, :])[:, :, None]
    qdec = jnp.exp(log_gamma[:, None] * (idx + 1.0)[None, :])[:, :, None]
    gam = jnp.broadcast_to(jnp.exp(cr * log_gamma)[:, None, None], (RET_HEADS, 1, LANES))
    dmat = _pad_rows(_pad_rows(dmat, TILE, 1), TILE, 2)
    return dmat, _pad_rows(qdec, TILE, 1), _pad_rows(kdec, TILE, 1), gam


def _pad_lanes(v):
    return jnp.pad(v, (0, LANES - v.shape[0]))[None, :]


def _tiles(n):
    return dict(tm_ffn=min(1024, n), tf=512, tm_out=min(512, n))


def _trunk(x, pos, rs0, gs0, cv0, p, emit_vn):
    b, t, d = x.shape
    n = b * t
    tl = _tiles(n)
    valid = min(t, TILE)
    tpad = -(-t // TILE) * TILE
    cosf, sinf = _rope_tables(pos)
    cosf, sinf = _pad_rows(cosf, tpad, 0), _pad_rows(sinf, tpad, 0)
    consts = _ret_consts(valid)
    x = x.reshape(n, d)
    outs = []
    for l in range(DEPTH):
        x = _ffn(x, p["norm_ffn1_w"][l][None], p["ffn1_w_up"], p["ffn1_w_down"], l, tl["tm_ffn"], tl["tf"])
        lp = {
            "alog": _pad_lanes(p["gdn_a_log"][l]), "dtb": _pad_lanes(p["gdn_dt_bias"][l]),
            "convw": p["gdn_conv_w"][l], "rgn": p["ret_gn_w"][l][None], "gnw": p["gdn_norm_w"][l][None],
            "mnw": p["gmlp_norm_w"][l][None],
            "ws": _pad_rows(_pad_rows(p["gmlp_ws"][l][:, :valid, :valid], TILE, 1), TILE, 2),
            "bs": _pad_rows(p["gmlp_bs"][l][:, :valid, None], TILE, 1),
        }
        res = _mixer(_pad_rows(x.reshape(b, t, d), tpad, 1), p["norm_mix_w"][l][None], p["w_in"], l,
                     cosf, sinf, consts, lp, rs0[l], gs0[l], cv0[l], valid, emit_vn)
        cat = res[0][:, :t].reshape(n, D_MIX)
        x = _outproj(x, cat, p["w_out"], l, tl["tm_out"])
        x = _ffn(x, p["norm_ffn2_w"][l][None], p["ffn2_w_up"], p["ffn2_w_down"], l, tl["tm_ffn"], tl["tf"],
                 final_nw=p["final_norm_w"][None] if l == DEPTH - 1 else None)
        outs.append(tuple(res[1:4]) + ((res[4][:, :t],) if emit_vn else ()))
    y = x.reshape(b, t, d)
    return (y,) + tuple(jnp.stack([o[i] for o in outs]) for i in range(len(outs[0])))


def _prep_params(p):
    q = dict(p)
    for k in ("ffn1_w_up", "ffn1_w_down", "ffn2_w_up", "ffn2_w_down", "w_out"):
        q[k] = p[k].astype(BF16)
    w = p["w_in"]
    ab0 = 2 * RET_QK + 2 * RET_V + GDN_CONV_CH + GDN_V
    wb = w.astype(BF16)
    q["w_in"] = jnp.concatenate(
        [wb[:, :, :ab0], wb[:, :, ab0 + 2 * GDN_HEADS:], wb[:, :, ab0:ab0 + 2 * GDN_HEADS],
         jnp.zeros(w.shape[:2] + (D_PROJ - D_IN,), BF16)], axis=-1)
    return q


def kernel(x_prompt, x_sample, state_ret, state_gdn, state_conv, norm_ffn1_w, ffn1_w_up, ffn1_w_down,
           norm_mix_w, w_in, ret_gn_w, gdn_conv_w, gdn_a_log, gdn_dt_bias, gdn_norm_w, gmlp_ws, gmlp_bs,
           gmlp_norm_w, w_out, norm_ffn2_w, ffn2_w_up, ffn2_w_down, final_norm_w):
    p = _prep_params({
        "norm_ffn1_w": norm_ffn1_w, "ffn1_w_up": ffn1_w_up, "ffn1_w_down": ffn1_w_down,
        "norm_mix_w": norm_mix_w, "w_in": w_in, "ret_gn_w": ret_gn_w, "gdn_conv_w": gdn_conv_w,
        "gdn_a_log": gdn_a_log, "gdn_dt_bias": gdn_dt_bias, "gdn_norm_w": gdn_norm_w,
        "gmlp_ws": gmlp_ws, "gmlp_bs": gmlp_bs, "gmlp_norm_w": gmlp_norm_w, "w_out": w_out,
        "norm_ffn2_w": norm_ffn2_w, "ffn2_w_up": ffn2_w_up, "ffn2_w_down": ffn2_w_down,
        "final_norm_w": final_norm_w})
    bp, tp = x_prompt.shape[0], x_prompt.shape[1]
    bs, ts = x_sample.shape[0], x_sample.shape[1]
    ret0 = jnp.zeros((DEPTH, bp, RET_HEADS, RET_DK, RET_DV), F32)
    gdn0 = jnp.zeros((DEPTH, bp, GDN_HEADS, GDN_DK, GDN_DV), F32)
    conv0 = jnp.zeros((DEPTH, bp, CONV_WIDTH - 1, GDN_CONV_CH), F32)
    pos_p = jnp.arange(tp, dtype=jnp.int32)
    y_p, ret_p, gdn_p, conv_p = _trunk(x_prompt, pos_p, ret0, gdn0, conv0, p, emit_vn=False)
    pos_s = PAST_LEN + jnp.arange(ts, dtype=jnp.int32)
    y_s, ret_s, gdn_s, conv_s, vn_s = _trunk(x_sample, pos_s, state_ret, state_gdn, state_conv, p, emit_vn=True)
    return (y_p, y_s, ret_p, ret_s, gdn_p, gdn_s, conv_p, conv_s, vn_s)
```

```python
import functools
import math

import jax
import jax.numpy as jnp
from jax import lax
from jax.experimental import pallas as pl
from jax.experimental.pallas import tpu as pltpu

F32 = jnp.float32
BF16 = jnp.bfloat16

D_MODEL = 2048
DEPTH = 4
PAST_LEN = 4096
EPS = 1e-6
ROPE_BASE = 10000.0
RET_HEADS = 6
RET_DK = 64
RET_DV = 128
GDN_HEADS = 6
GDN_DK = 128
GDN_DV = 128
CONV_WIDTH = 4
GMLP_GROUPS = 4
GMLP_DIM = 128
GMLP_CHUNK = 128
D_FF = 5632

RET_QK = RET_HEADS * RET_DK
RET_V = RET_HEADS * RET_DV
GDN_QK = GDN_HEADS * GDN_DK
GDN_V = GDN_HEADS * GDN_DV
GDN_CONV_CH = 2 * GDN_QK + GDN_V
GMLP_W = GMLP_GROUPS * GMLP_DIM
D_MIX = RET_V + GDN_V + GMLP_W
D_IN = RET_QK * 2 + RET_V * 2 + GDN_CONV_CH + GDN_V + GDN_HEADS * 2 + GMLP_W * 2

LANES = 128
TILE = 128
INV_BLOCK = 16
PROJ_COLS = 256
PROJ_K = 512
FILL_CONV, FILL_STAGE1, FILL_CHAIN, FILL_SOLVE, FILL_RECUR = 0, 5, 4, 2, 6

OFF_RQ = 0
OFF_RK = OFF_RQ + RET_QK
OFF_RV = OFF_RK + RET_QK
OFF_RG = OFF_RV + RET_V
OFF_GQKV = OFF_RG + RET_V
OFF_GZ = OFF_GQKV + GDN_CONV_CH
OFF_CU = OFF_GZ + GDN_V
OFF_CV = OFF_CU + GMLP_W
OFF_AB = OFF_CV + GMLP_W
D_PROJ = 6656

VMEM_LIMIT = 56 * 1024 * 1024


def _cparams(sem):
    return pltpu.CompilerParams(dimension_semantics=sem, vmem_limit_bytes=VMEM_LIMIT)


def _dot(a, b):
    return jnp.dot(a, b, preferred_element_type=F32)


def _dot_nt(a, b):
    return lax.dot_general(a, b, (((1,), (1,)), ((), ())), preferred_element_type=F32)


def _dot_tn(a, b):
    return lax.dot_general(a, b, (((0,), (0,)), ((), ())), preferred_element_type=F32)


def _silu(x):
    return x * jax.nn.sigmoid(x)


def _rms_rows(x, w):
    return x * lax.rsqrt(jnp.mean(x * x, axis=-1, keepdims=True) + EPS) * w


def _ffn_kernel(x_ref, nw_ref, wg_ref, wu_ref, wd_ref, *rest):
    fnw_ref = rest[0] if len(rest) == 3 else None
    o_ref, xn_ref = rest[-2:]
    j = pl.program_id(1)

    def delta():
        xn = xn_ref[...]
        g = _dot(xn, wg_ref[...])
        u = _dot(xn, wu_ref[...])
        a = (0.5 * _silu(g) * u).astype(BF16)
        return _dot(a, wd_ref[...])

    @pl.when(j == 0)
    def _():
        xn_ref[...] = _rms_rows(x_ref[...], nw_ref[...]).astype(BF16)
        o_ref[...] = x_ref[...] + delta()

    @pl.when(j != 0)
    def _():
        o_ref[...] += delta()

    if fnw_ref is not None:
        @pl.when(j == pl.num_programs(1) - 1)
        def _():
            o_ref[...] = _rms_rows(o_ref[...], fnw_ref[...])


def _ffn(x, nw, w_up, w_down, l, tm, tf, final_nw=None):
    n, d = x.shape
    dff = w_down.shape[1]
    nf = dff // tf
    vec = pl.BlockSpec((1, d), lambda i, j: (0, 0))
    return pl.pallas_call(
        _ffn_kernel,
        grid=(n // tm, nf),
        in_specs=[
            pl.BlockSpec((tm, d), lambda i, j: (i, 0)),
            vec,
            pl.BlockSpec((None, d, tf), lambda i, j: (l, 0, j)),
            pl.BlockSpec((None, d, tf), lambda i, j: (l, 0, j + nf)),
            pl.BlockSpec((None, tf, d), lambda i, j: (l, j, 0)),
        ] + ([] if final_nw is None else [vec]),
        out_specs=pl.BlockSpec((tm, d), lambda i, j: (i, 0)),
        out_shape=jax.ShapeDtypeStruct((n, d), F32),
        scratch_shapes=[pltpu.VMEM((tm, d), BF16)],
        compiler_params=_cparams(("parallel", "arbitrary")),
        name="ffn",
    )(x, nw, w_up, w_up, w_down, *(() if final_nw is None else (final_nw,)))


def _outproj_kernel(x_ref, c_ref, w_ref, o_ref):
    o_ref[...] = x_ref[...] + _dot(c_ref[...], w_ref[...])


def _outproj(x, cat, w, l, tm):
    n, d = x.shape
    k = cat.shape[1]
    return pl.pallas_call(
        _outproj_kernel,
        grid=(n // tm,),
        in_specs=[
            pl.BlockSpec((tm, d), lambda i: (i, 0)),
            pl.BlockSpec((tm, k), lambda i: (i, 0)),
            pl.BlockSpec((None, k, d), lambda i: (l, 0, 0)),
        ],
        out_specs=pl.BlockSpec((tm, d), lambda i: (i, 0)),
        out_shape=jax.ShapeDtypeStruct((n, d), F32),
        compiler_params=_cparams(("parallel",)),
        name="outproj",
    )(x, cat, w)


def _split3(x):
    hi = x.astype(BF16)
    r1 = x - hi.astype(F32)
    mid = r1.astype(BF16)
    lo = (r1 - mid.astype(F32)).astype(BF16)
    return hi, mid, lo


def _mixer_kernel(valid, emit_vn, pipelined,
                  x0_ref, x1_ref, nw_ref, win_ref,
                  cos_ref, sin_ref, dmat_ref, qdec_ref, kdec_ref, gam_ref,
                  alog_ref, dtb_ref, convw_ref, rgn_ref, gnw_ref, mnw_ref, ws_ref, bs_ref,
                  rs0_ref, gs0_ref, cv0_ref,
                  cat_ref, rs_ref, gs_ref, cvn_ref, *rest):
    if emit_vn:
        vn_ref, proj_s, xn_s, sret, sgdn, xbuf, gq_s, gk_s, gv_s, uv_s, a1_s, a2_s = rest
    else:
        vn_ref = None
        proj_s, xn_s, sret, sgdn, xbuf, gq_s, gk_s, gv_s, uv_s, a1_s, a2_s = rest
    tt = TILE
    t = pl.program_id(1)
    nt = pl.num_programs(1)
    cur = t % 2
    heads = range(GDN_HEADS)
    col_blocks = [slice(PROJ_COLS * c, PROJ_COLS * (c + 1)) for c in range(D_PROJ // PROJ_COLS)]

    @pl.when(t == 0)
    def _():
        for h in range(RET_HEADS):
            r = (h % 2) * RET_DK
            sret[h] = jnp.zeros((2 * RET_DK, RET_DV), F32)
            sret[h, r:r + RET_DK, :] = rs0_ref[0, h]
        for h in heads:
            sgdn[h] = gs0_ref[0, h]
        xbuf[0:5, :] = jnp.zeros((5, GDN_CONV_CH), F32)
        xbuf[5:8, :] = cv0_ref[0]
        xn0 = _rms_rows(x0_ref[0], nw_ref[...]).astype(BF16)
        for cs in col_blocks:
            proj_s[0, :, cs] = _dot(xn0, win_ref[:, cs])

    def proj(off, width):
        return proj_s[cur, :, off:off + width]

    lane = lax.broadcasted_iota(jnp.int32, (tt, LANES), 1)
    ri = lax.broadcasted_iota(jnp.int32, (tt, tt), 0)
    ci = lax.broadcasted_iota(jnp.int32, (tt, tt), 1)
    tri = ri >= ci
    strict = ri > ci
    eye = jnp.where(ri == ci, 1.0, 0.0).astype(F32)

    def project_next(cs, kc):
        ks = slice(PROJ_K * kc, PROJ_K * (kc + 1))
        part = _dot(xn_s[:, ks], win_ref[ks, cs])
        if kc == 0:
            proj_s[1 - cur, :, cs] = part
        else:
            proj_s[1 - cur, :, cs] += part

    first_half = (lane % RET_DK) < (RET_DK // 2)

    def rope(x):
        rot = jnp.where(first_half, pltpu.roll(x, LANES - RET_DK // 2, 1), pltpu.roll(x, RET_DK // 2, 1))
        return x * cos_ref[...] + rot * sin_ref[...]

    def retention_pair(j):
        qp = rope(proj(OFF_RQ + LANES * j, LANES))
        kp = rope(proj(OFF_RK + LANES * j, LANES)) * (RET_DK ** -0.5)
        qpb = qp.astype(BF16)
        for e in range(2):
            h = 2 * j + e
            km = jnp.where(lane >= RET_DK if e else lane < RET_DK, kp, 0.0)
            v = proj(OFF_RV + RET_DV * h, RET_DV).astype(BF16)
            s_prev = sret[h]
            scores = _dot_nt(qpb, km.astype(BF16)) * dmat_ref[h]
            lhs = jnp.concatenate([scores.astype(BF16), (qp * qdec_ref[h]).astype(BF16)], axis=1)
            o = _dot(lhs, jnp.concatenate([v, s_prev.astype(BF16)], axis=0))
            sret[h] = gam_ref[h] * s_prev + _dot_tn((km * kdec_ref[h]).astype(BF16), v)
            mu = jnp.mean(o, axis=-1, keepdims=True)
            oc = o - mu
            var = jnp.mean(oc * oc, axis=-1, keepdims=True)
            y = oc * lax.rsqrt(var + EPS) * rgn_ref[:, RET_DV * h:RET_DV * (h + 1)]
            cat_ref[0, :, RET_DV * h:RET_DV * (h + 1)] = (y * _silu(proj(OFF_RG + RET_DV * h, RET_DV))).astype(BF16)

    def gmlp_group(g):
        cs = slice(GMLP_DIM * g, GMLP_DIM * (g + 1))
        v = proj(OFF_CV + GMLP_DIM * g, GMLP_DIM)
        mu = jnp.mean(v, axis=-1, keepdims=True)
        vc = v - mu
        var = jnp.mean(vc * vc, axis=-1, keepdims=True)
        vg = vc * lax.rsqrt(var + EPS) * mnw_ref[:, cs]
        if emit_vn:
            vn_ref[0, :, cs] = vg
        wtri = jnp.where(tri, ws_ref[g], 0.0).astype(BF16)
        z = _dot(wtri, vg.astype(BF16)) + bs_ref[g]
        u = proj(OFF_CU + GMLP_DIM * g, GMLP_DIM)
        cat_ref[0, :, RET_V + GDN_V + GMLP_DIM * g:RET_V + GDN_V + GMLP_DIM * (g + 1)] = (u * z).astype(BF16)

    side = [functools.partial(retention_pair, j) for j in range(RET_HEADS // 2)]
    side += [functools.partial(gmlp_group, g) for g in range(GMLP_GROUPS)]
    dense = ([functools.partial(project_next, cs, kc) for cs in col_blocks for kc in range(D_MODEL // PROJ_K)]
             if pipelined else [])

    def fill(n_dense, n_side=0):
        for _ in range(n_dense):
            if dense:
                dense.pop(0)()
        for _ in range(n_side):
            if side:
                side.pop(0)()

    if pipelined:
        xn_s[...] = _rms_rows(x1_ref[0], nw_ref[...]).astype(BF16)

    xbuf[8:8 + tt, :] = proj(OFF_GQKV, GDN_CONV_CH)
    for p in range(GDN_CONV_CH // LANES):
        cs = slice(LANES * p, LANES * (p + 1))
        xc = xbuf[:, cs]
        y = xc * convw_ref[0:1, cs]
        for i in range(1, CONV_WIDTH):
            y = pltpu.roll(y, 1, 0) + xc * convw_ref[i:i + 1, cs]
        y = _silu(y[8:8 + tt])
        kind, h = divmod(p, GDN_HEADS)
        if kind == 2:
            gv_s[h] = y
        else:
            yn = y * lax.rsqrt(jnp.sum(y * y, axis=-1, keepdims=True) + EPS)
            if kind == 0:
                gq_s[h] = yn * (GDN_DK ** -0.5)
            else:
                gk_s[h] = yn
        fill(FILL_CONV)
    tail = xbuf[5 + valid:8 + valid, :]
    xbuf[5:8, :] = tail
    cvn_ref[0] = tail

    ab = proj(OFF_AB, LANES)
    log_a = -jnp.exp(alog_ref[...]) * jax.nn.softplus(ab + dtb_ref[...])
    beta = jax.nn.sigmoid(ab)
    if valid < tt:
        real = lax.broadcasted_iota(jnp.int32, (tt, LANES), 0) < valid
        log_a = jnp.where(real, log_a, 0.0)
        beta = jnp.where(real, beta, 0.0)
    cum_mat = jnp.where(tri, 1.0, 0.0).astype(BF16)
    la_hi, la_mid, la_lo = _split3(log_a)
    gcum = (_dot(cum_mat, la_hi) + _dot(cum_mat, la_mid)) + _dot(cum_mat, la_lo)
    gcum_t = gcum.T

    ls = []
    for h in heads:
        gcol = gcum[:, h:h + 1]
        grow = gcum_t[h:h + 1, :]
        decay = jnp.exp(jnp.where(tri, gcol - grow, -1e30))
        bcol = beta[:, GDN_HEADS + h:GDN_HEADS + h + 1]
        kb = gk_s[h].astype(BF16)
        r = _dot_nt(jnp.concatenate([gq_s[h].astype(BF16), kb], axis=0), kb)
        a2_s[h, 0:tt, :] = (r[:tt] * decay).astype(BF16)
        ls.append(bcol * r[tt:] * jnp.where(strict, decay, 0.0))
        fill(FILL_STAGE1)

    zero_b = jnp.zeros((tt, tt), BF16)

    def pair_dot(a2, b2):
        bb = b2.astype(BF16)
        rhs = jnp.concatenate([jnp.concatenate([bb[:, :tt], zero_b], axis=1),
                               jnp.concatenate([zero_b, bb[:, tt:]], axis=1)], axis=0)
        return _dot(a2.astype(BF16), rhs)

    def same_block(b):
        return (ri // b) == (ci // b)

    pairs = range(GDN_HEADS // 2)
    eye2 = jnp.concatenate([eye, eye], axis=1)
    l2 = [jnp.concatenate([ls[2 * j], ls[2 * j + 1]], axis=1) for j in pairs]
    diag = same_block(INV_BLOCK)
    diag2 = jnp.concatenate([diag, diag], axis=1)
    pw = [jnp.where(diag2, -l2[j], 0.0) for j in pairs]
    td = [eye2 + pw[j] for j in pairs]
    n_fac = max(int(math.ceil(math.log2(min(INV_BLOCK, valid)))), 1)
    for it in range(n_fac):
        if it == 0:
            if n_fac > 1:
                pw = [pair_dot(pw[j], pw[j]) for j in pairs]
        elif it == n_fac - 1:
            td = [td[j] + pair_dot(td[j], pw[j]) for j in pairs]
        else:
            r = [pair_dot(jnp.concatenate([pw[j], td[j]], axis=0), pw[j]) for j in pairs]
            pw = [r[j][:tt] for j in pairs]
            td = [td[j] + r[j][tt:] for j in pairs]
        fill(FILL_CHAIN, 1)
    blk = INV_BLOCK
    while blk < valid:
        off = same_block(2 * blk) & jnp.logical_not(same_block(blk))
        off2 = jnp.concatenate([off, off], axis=1)
        y = [pair_dot(jnp.where(off2, l2[j], 0.0), td[j]) for j in pairs]
        fill(FILL_CHAIN, 1)
        z = [pair_dot(td[j], y[j]) for j in pairs]
        td = [td[j] - z[j] for j in pairs]
        fill(FILL_CHAIN)
        blk *= 2

    gls = []
    for h in heads:
        j, e = divmod(h, 2)
        tinv = td[j][:, tt * e:tt * (e + 1)]
        gcol = gcum[:, h:h + 1]
        eg = jnp.exp(gcol)
        bcol = beta[:, GDN_HEADS + h:GDN_HEADS + h + 1]
        k = gk_s[h]
        rhs = jnp.concatenate([bcol * gv_s[h], (bcol * eg) * k], axis=1).astype(BF16)
        sol = _dot(tinv.astype(BF16), rhs)
        uv_s[h] = sol[:, :GDN_DV]
        a1_s[h, 0:tt, :] = sol[:, GDN_DV:].astype(BF16)
        a1_s[h, tt:2 * tt, :] = (gq_s[h] * eg).astype(BF16)
        glast = gcum[tt - 1:tt, h:h + 1]
        a2_s[h, tt:2 * tt, :] = (k * jnp.exp(glast - gcol)).T.astype(BF16)
        gls.append(jnp.exp(glast))
        fill(FILL_SOLVE)

    ss = [sgdn[h] for h in heads]
    r1 = [_dot(a1_s[h], ss[h].astype(BF16)) for h in heads]
    fill(FILL_RECUR)
    ub = [(uv_s[h] - r1[h][:tt]).astype(BF16) for h in heads]
    r2 = [_dot(a2_s[h], ub[h]) for h in heads]
    fill(FILL_RECUR)
    for h in heads:
        sgdn[h] = ss[h] * gls[h] + r2[h][tt:]
        o = r1[h][tt:] + r2[h][:tt]
        y = o * lax.rsqrt(jnp.mean(o * o, axis=-1, keepdims=True) + EPS) * gnw_ref[...]
        cat_ref[0, :, RET_V + GDN_DV * h:RET_V + GDN_DV * (h + 1)] = (y * _silu(proj(OFF_GZ + GDN_DV * h, GDN_DV))).astype(BF16)

    fill(len(dense), len(side))

    @pl.when(t == nt - 1)
    def _():
        for h in range(RET_HEADS):
            r = (h % 2) * RET_DK
            rs_ref[0, h] = sret[h, r:r + RET_DK, :]
        for h in heads:
            gs_ref[0, h] = sgdn[h]


def _mixer(x, nw, w_in, l, cosf, sinf, consts, lp, rs0, gs0, cv0, valid, emit_vn):
    b, t, d = x.shape
    tt = TILE
    nt = t // tt
    dmat, qdec, kdec, gam = consts
    full = lambda a: pl.BlockSpec(a.shape, lambda i, j, _n=a.ndim: (0,) * _n)
    small = [cosf, sinf, dmat, qdec, kdec, gam,
             lp["alog"], lp["dtb"], lp["convw"], lp["rgn"], lp["gnw"], lp["mnw"], lp["ws"], lp["bs"]]
    state_specs = [
        pl.BlockSpec((1, RET_HEADS, RET_DK, RET_DV), lambda i, j: (i, 0, 0, 0)),
        pl.BlockSpec((1, GDN_HEADS, GDN_DK, GDN_DV), lambda i, j: (i, 0, 0, 0)),
        pl.BlockSpec((1, CONV_WIDTH - 1, GDN_CONV_CH), lambda i, j: (i, 0, 0)),
    ]
    in_specs = [
        pl.BlockSpec((1, tt, d), lambda i, j: (i, 0, 0)),
        pl.BlockSpec((1, tt, d), lambda i, j: (i, jnp.minimum(j + 1, nt - 1), 0)),
        pl.BlockSpec((1, d), lambda i, j: (0, 0)),
        pl.BlockSpec((None, d, D_PROJ), lambda i, j: (l, 0, 0), pipeline_mode=pl.Buffered(1)),
        pl.BlockSpec((tt, LANES), lambda i, j: (j, 0)),
        pl.BlockSpec((tt, LANES), lambda i, j: (j, 0)),
    ] + [full(a) for a in small[2:]] + state_specs
    out_shape = [
        jax.ShapeDtypeStruct((b, t, D_MIX), BF16),
        jax.ShapeDtypeStruct((b, RET_HEADS, RET_DK, RET_DV), F32),
        jax.ShapeDtypeStruct((b, GDN_HEADS, GDN_DK, GDN_DV), F32),
        jax.ShapeDtypeStruct((b, CONV_WIDTH - 1, GDN_CONV_CH), F32),
    ]
    out_specs = [pl.BlockSpec((1, tt, D_MIX), lambda i, j: (i, j, 0))] + state_specs
    if emit_vn:
        out_shape.append(jax.ShapeDtypeStruct((b, t, GMLP_W), F32))
        out_specs.append(pl.BlockSpec((1, tt, GMLP_W), lambda i, j: (i, j, 0)))
    hs = (GDN_HEADS, tt, LANES)
    scratch = [
        pltpu.VMEM((2, tt, D_PROJ), F32),
        pltpu.VMEM((tt, d), BF16),
        pltpu.VMEM((RET_HEADS, 2 * RET_DK, RET_DV), F32),
        pltpu.VMEM((GDN_HEADS, GDN_DK, GDN_DV), F32),
        pltpu.VMEM((tt + 8, GDN_CONV_CH), F32),
        pltpu.VMEM(hs, F32), pltpu.VMEM(hs, F32), pltpu.VMEM(hs, F32),
        pltpu.VMEM(hs, F32),
        pltpu.VMEM((GDN_HEADS, 2 * tt, LANES), BF16),
        pltpu.VMEM((GDN_HEADS, 2 * tt, LANES), BF16),
    ]
    return pl.pallas_call(
        functools.partial(_mixer_kernel, valid, emit_vn, nt > 1),
        grid=(b, nt),
        in_specs=in_specs,
        out_specs=out_specs,
        out_shape=out_shape,
        scratch_shapes=scratch,
        compiler_params=_cparams(("parallel", "arbitrary")),
        name="mixer",
    )(x, x, nw, w_in, *small, rs0, gs0, cv0)


def _pad_rows(a, rows, axis):
    pad = [(0, 0)] * a.ndim
    pad[axis] = (0, rows - a.shape[axis])
    return jnp.pad(a, pad)


def _rope_tables(pos):
    half = RET_DK // 2
    inv = jnp.power(ROPE_BASE, -jnp.arange(half, dtype=F32) / half)
    ang = pos.astype(F32)[:, None] * inv[None, :]
    cos, sin = jnp.cos(ang), jnp.sin(ang)
    reps = LANES // RET_DK
    return jnp.tile(jnp.concatenate([cos, cos], axis=1), (1, reps)), jnp.tile(jnp.concatenate([-sin, sin], axis=1), (1, reps))


def _ret_consts(cr):
    log_gamma = jnp.log1p(-jnp.power(2.0, -5.0 - jnp.arange(RET_HEADS, dtype=F32)))
    idx = jnp.arange(cr, dtype=F32)
    diff = idx[:, None] - idx[None, :]
    causal = diff >= 0
    dmat = jnp.where(causal[None], jnp.exp(jnp.where(causal, diff, 0.0)[None] * log_gamma[:, None, None]), 0.0)
    kdec = jnp.exp(log_gamma[:, None] * (cr - 1 - idx)[None, :])[:, :, None]
    qdec = jnp.exp(log_gamma[:, None] * (idx + 1.0)[None, :])[:, :, None]
    gam = jnp.broadcast_to(jnp.exp(cr * log_gamma)[:, None, None], (RET_HEADS, 1, LANES))
    dmat = _pad_rows(_pad_rows(dmat, TILE, 1), TILE, 2)
    return dmat, _pad_rows(qdec, TILE, 1), _pad_rows(kdec, TILE, 1), gam


def _pad_lanes(v):
    return jnp.pad(v, (0, LANES - v.shape[0]))[None, :]


def _tiles(n):
    return dict(tm_ffn=min(1024, n), tf=512, tm_out=min(512, n))


def _trunk(x, pos, rs0, gs0, cv0, p, emit_vn):
    b, t, d = x.shape
    n = b * t
    tl = _tiles(n)
    valid = min(t, TILE)
    tpad = -(-t // TILE) * TILE
    cosf, sinf = _rope_tables(pos)
    cosf, sinf = _pad_rows(cosf, tpad, 0), _pad_rows(sinf, tpad, 0)
    consts = _ret_consts(valid)
    x = x.reshape(n, d)
    outs = []
    for l in range(DEPTH):
        x = _ffn(x, p["norm_ffn1_w"][l][None], p["ffn1_w_up"], p["ffn1_w_down"], l, tl["tm_ffn"], tl["tf"])
        lp = {
            "alog": _pad_lanes(p["gdn_a_log"][l]), "dtb": _pad_lanes(p["gdn_dt_bias"][l]),
            "convw": p["gdn_conv_w"][l], "rgn": p["ret_gn_w"][l][None], "gnw": p["gdn_norm_w"][l][None],
            "mnw": p["gmlp_norm_w"][l][None],
            "ws": _pad_rows(_pad_rows(p["gmlp_ws"][l][:, :valid, :valid], TILE, 1), TILE, 2),
            "bs": _pad_rows(p["gmlp_bs"][l][:, :valid, None], TILE, 1),
        }
        res = _mixer(_pad_rows(x.reshape(b, t, d), tpad, 1), p["norm_mix_w"][l][None], p["w_in"], l,
                     cosf, sinf, consts, lp, rs0[l], gs0[l], cv0[l], valid, emit_vn)
        cat = res[0][:, :t].reshape(n, D_MIX)
        x = _outproj(x, cat, p["w_out"], l, tl["tm_out"])
        x = _ffn(x, p["norm_ffn2_w"][l][None], p["ffn2_w_up"], p["ffn2_w_down"], l, tl["tm_ffn"], tl["tf"],
                 final_nw=p["final_norm_w"][None] if l == DEPTH - 1 else None)
        outs.append(tuple(res[1:4]) + ((res[4][:, :t],) if emit_vn else ()))
    y = x.reshape(b, t, d)
    return (y,) + tuple(jnp.stack([o[i] for o in outs]) for i in range(len(outs[0])))


def _win_kernel(w_ref, o_ref):
    ab0 = OFF_CU
    nab = 2 * GDN_HEADS
    o_ref[:, :ab0] = w_ref[:, :ab0].astype(BF16)
    o_ref[:, OFF_CU:OFF_AB] = w_ref[:, ab0 + nab:].astype(BF16)
    o_ref[:, OFF_AB:] = jnp.zeros((o_ref.shape[0], D_PROJ - OFF_AB), BF16)
    o_ref[:, OFF_AB:OFF_AB + nab] = w_ref[:, ab0:ab0 + nab].astype(BF16)


def _relayout_w_in(w, tr=256):
    depth, d, d_in = w.shape
    return pl.pallas_call(
        _win_kernel,
        grid=(depth, d // tr),
        in_specs=[pl.BlockSpec((None, tr, d_in), lambda l, i: (l, i, 0))],
        out_specs=pl.BlockSpec((None, tr, D_PROJ), lambda l, i: (l, i, 0)),
        out_shape=jax.ShapeDtypeStruct((depth, d, D_PROJ), BF16),
        compiler_params=_cparams(("parallel", "parallel")),
        name="w_in_layout",
    )(w)


def _prep_params(p):
    q = dict(p)
    for k in ("ffn1_w_up", "ffn1_w_down", "ffn2_w_up", "ffn2_w_down", "w_out"):
        q[k] = p[k].astype(BF16)
    q["w_in"] = _relayout_w_in(p["w_in"])
    return q


def kernel(x_prompt, x_sample, state_ret, state_gdn, state_conv, norm_ffn1_w, ffn1_w_up, ffn1_w_down,
           norm_mix_w, w_in, ret_gn_w, gdn_conv_w, gdn_a_log, gdn_dt_bias, gdn_norm_w, gmlp_ws, gmlp_bs,
           gmlp_norm_w, w_out, norm_ffn2_w, ffn2_w_up, ffn2_w_down, final_norm_w):
    p = _prep_params({
        "norm_ffn1_w": norm_ffn1_w, "ffn1_w_up": ffn1_w_up, "ffn1_w_down": ffn1_w_down,
        "norm_mix_w": norm_mix_w, "w_in": w_in, "ret_gn_w": ret_gn_w, "gdn_conv_w": gdn_conv_w,
        "gdn_a_log": gdn_a_log, "gdn_dt_bias": gdn_dt_bias, "gdn_norm_w": gdn_norm_w,
        "gmlp_ws": gmlp_ws, "gmlp_bs": gmlp_bs, "gmlp_norm_w": gmlp_norm_w, "w_out": w_out,
        "norm_ffn2_w": norm_ffn2_w, "ffn2_w_up": ffn2_w_up, "ffn2_w_down": ffn2_w_down,
        "final_norm_w": final_norm_w})
    bp, tp = x_prompt.shape[0], x_prompt.shape[1]
    bs, ts = x_sample.shape[0], x_sample.shape[1]
    ret0 = jnp.zeros((DEPTH, bp, RET_HEADS, RET_DK, RET_DV), F32)
    gdn0 = jnp.zeros((DEPTH, bp, GDN_HEADS, GDN_DK, GDN_DV), F32)
    conv0 = jnp.zeros((DEPTH, bp, CONV_WIDTH - 1, GDN_CONV_CH), F32)
    pos_p = jnp.arange(tp, dtype=jnp.int32)
    y_p, ret_p, gdn_p, conv_p = _trunk(x_prompt, pos_p, ret0, gdn0, conv0, p, emit_vn=False)
    pos_s = PAST_LEN + jnp.arange(ts, dtype=jnp.int32)
    y_s, ret_s, gdn_s, conv_s, vn_s = _trunk(x_sample, pos_s, state_ret, state_gdn, state_conv, p, emit_vn=True)
    return (y_p, y_s, ret_p, ret_s, gdn_p, gdn_s, conv_p, conv_s, vn_s)
```

```python
import functools
import math

import jax
import jax.numpy as jnp
from jax import lax
from jax.experimental import pallas as pl
from jax.experimental.pallas import tpu as pltpu

F32 = jnp.float32
BF16 = jnp.bfloat16

D_MODEL = 2048
DEPTH = 4
PAST_LEN = 4096
EPS = 1e-6
ROPE_BASE = 10000.0
RET_HEADS = 6
RET_DK = 64
RET_DV = 128
GDN_HEADS = 6
GDN_DK = 128
GDN_DV = 128
CONV_WIDTH = 4
GMLP_GROUPS = 4
GMLP_DIM = 128
GMLP_CHUNK = 128
D_FF = 5632

RET_QK = RET_HEADS * RET_DK
RET_V = RET_HEADS * RET_DV
GDN_QK = GDN_HEADS * GDN_DK
GDN_V = GDN_HEADS * GDN_DV
GDN_CONV_CH = 2 * GDN_QK + GDN_V
GMLP_W = GMLP_GROUPS * GMLP_DIM
D_MIX = RET_V + GDN_V + GMLP_W
D_IN = RET_QK * 2 + RET_V * 2 + GDN_CONV_CH + GDN_V + GDN_HEADS * 2 + GMLP_W * 2

LANES = 128
TILE = 128
INV_BLOCK = 16
PROJ_COLS = 256
PROJ_K = 2048
FILL_SERIES, FILL_MERGE, FILL_SOLVE, FILL_RECUR = (3, 0, 1), (2, 3, 1), (0, 1, 0), (1, 1, 0)

OFF_RQ = 0
OFF_RK = OFF_RQ + RET_QK
OFF_RV = OFF_RK + RET_QK
OFF_RG = OFF_RV + RET_V
OFF_GQKV = OFF_RG + RET_V
OFF_GZ = OFF_GQKV + GDN_CONV_CH
OFF_CU = OFF_GZ + GDN_V
OFF_CV = OFF_CU + GMLP_W
OFF_AB = OFF_CV + GMLP_W
D_PROJ = 6656

VMEM_LIMIT = 56 * 1024 * 1024


def _cparams(sem):
    return pltpu.CompilerParams(dimension_semantics=sem, vmem_limit_bytes=VMEM_LIMIT)


def _dot(a, b):
    return jnp.dot(a, b, preferred_element_type=F32)


def _dot_nt(a, b):
    return lax.dot_general(a, b, (((1,), (1,)), ((), ())), preferred_element_type=F32)


def _dot_tn(a, b):
    return lax.dot_general(a, b, (((0,), (0,)), ((), ())), preferred_element_type=F32)


def _silu(x):
    return x * jax.nn.sigmoid(x)


def _rms_rows(x, w):
    return x * lax.rsqrt(jnp.mean(x * x, axis=-1, keepdims=True) + EPS) * w


def _ffn_kernel(x_ref, nw_ref, wg_ref, wu_ref, wd_ref, *rest):
    fnw_ref = rest[0] if len(rest) == 3 else None
    o_ref, xn_ref = rest[-2:]
    j = pl.program_id(1)

    def delta():
        xn = xn_ref[...]
        g = _dot(xn, wg_ref[...])
        u = _dot(xn, wu_ref[...])
        a = (0.5 * _silu(g) * u).astype(BF16)
        return _dot(a, wd_ref[...])

    @pl.when(j == 0)
    def _():
        xn_ref[...] = _rms_rows(x_ref[...], nw_ref[...]).astype(BF16)
        o_ref[...] = x_ref[...] + delta()

    @pl.when(j != 0)
    def _():
        o_ref[...] += delta()

    if fnw_ref is not None:
        @pl.when(j == pl.num_programs(1) - 1)
        def _():
            o_ref[...] = _rms_rows(o_ref[...], fnw_ref[...])


def _ffn(x, nw, w_up, w_down, l, tm, tf, final_nw=None):
    n, d = x.shape
    dff = w_down.shape[1]
    nf = dff // tf
    vec = pl.BlockSpec((1, d), lambda i, j: (0, 0))
    return pl.pallas_call(
        _ffn_kernel,
        grid=(n // tm, nf),
        in_specs=[
            pl.BlockSpec((tm, d), lambda i, j: (i, 0)),
            vec,
            pl.BlockSpec((None, d, tf), lambda i, j: (l, 0, j)),
            pl.BlockSpec((None, d, tf), lambda i, j: (l, 0, j + nf)),
            pl.BlockSpec((None, tf, d), lambda i, j: (l, j, 0)),
        ] + ([] if final_nw is None else [vec]),
        out_specs=pl.BlockSpec((tm, d), lambda i, j: (i, 0)),
        out_shape=jax.ShapeDtypeStruct((n, d), F32),
        scratch_shapes=[pltpu.VMEM((tm, d), BF16)],
        compiler_params=_cparams(("parallel", "arbitrary")),
        name="ffn",
    )(x, nw, w_up, w_up, w_down, *(() if final_nw is None else (final_nw,)))


def _outproj_kernel(x_ref, c_ref, w_ref, o_ref):
    o_ref[...] = x_ref[...] + _dot(c_ref[...], w_ref[...])


def _outproj(x, cat, w, l, tm):
    n, d = x.shape
    k = cat.shape[1]
    return pl.pallas_call(
        _outproj_kernel,
        grid=(n // tm,),
        in_specs=[
            pl.BlockSpec((tm, d), lambda i: (i, 0)),
            pl.BlockSpec((tm, k), lambda i: (i, 0)),
            pl.BlockSpec((None, k, d), lambda i: (l, 0, 0)),
        ],
        out_specs=pl.BlockSpec((tm, d), lambda i: (i, 0)),
        out_shape=jax.ShapeDtypeStruct((n, d), F32),
        compiler_params=_cparams(("parallel",)),
        name="outproj",
    )(x, cat, w)


def _split3(x):
    hi = x.astype(BF16)
    r1 = x - hi.astype(F32)
    mid = r1.astype(BF16)
    lo = (r1 - mid.astype(F32)).astype(BF16)
    return hi, mid, lo


def _mixer_kernel(valid, emit_vn, pipelined,
                  x0_ref, x1_ref, nw_ref, win_ref,
                  cos_ref, sin_ref, dmat_ref, qdec_ref, kdec_ref, gam_ref,
                  alog_ref, dtb_ref, convw_ref, rgn_ref, gnw_ref, mnw_ref, ws_ref, bs_ref,
                  rs0_ref, gs0_ref, cv0_ref,
                  cat_ref, rs_ref, gs_ref, cvn_ref, *rest):
    if emit_vn:
        vn_ref = rest[0]
        rest = rest[1:]
    else:
        vn_ref = None
    proj_s, xn_s, sret, sgdn, xbuf, gq_s, gk_s, gv_s, l_s, gcum_s, beta_s, uv_s, a1_s, a2_s = rest
    tt = TILE
    t = pl.program_id(1)
    nt = pl.num_programs(1)
    cur = t % 2
    heads = range(GDN_HEADS)
    col_blocks = [slice(PROJ_COLS * c, PROJ_COLS * (c + 1)) for c in range(D_PROJ // PROJ_COLS)]

    lane = lax.broadcasted_iota(jnp.int32, (tt, LANES), 1)
    ri = lax.broadcasted_iota(jnp.int32, (tt, tt), 0)
    ci = lax.broadcasted_iota(jnp.int32, (tt, tt), 1)
    tri = ri >= ci
    strict = ri > ci
    eye = jnp.where(ri == ci, 1.0, 0.0).astype(F32)

    def conv_load(slot, carry):
        if carry:
            xbuf[5:8, :] = xbuf[5 + tt:8 + tt, :]
        xbuf[8:8 + tt, :] = proj_s[slot, :, OFF_GQKV:OFF_GQKV + GDN_CONV_CH]
        cvn_ref[0] = xbuf[5 + valid:8 + valid, :]

    def conv_piece(slot, p):
        cs = slice(LANES * p, LANES * (p + 1))
        xc = xbuf[:, cs]
        y = xc * convw_ref[0:1, cs]
        for i in range(1, CONV_WIDTH):
            y = pltpu.roll(y, 1, 0) + xc * convw_ref[i:i + 1, cs]
        y = _silu(y[8:8 + tt])
        kind, h = divmod(p, GDN_HEADS)
        if kind == 2:
            gv_s[slot, h] = y
        else:
            yn = y * lax.rsqrt(jnp.sum(y * y, axis=-1, keepdims=True) + EPS)
            if kind == 0:
                gq_s[slot, h] = yn * (GDN_DK ** -0.5)
            else:
                gk_s[slot, h] = yn

    def gates(slot, env):
        ab = proj_s[slot, :, OFF_AB:OFF_AB + LANES]
        log_a = -jnp.exp(alog_ref[...]) * jax.nn.softplus(ab + dtb_ref[...])
        beta = jax.nn.sigmoid(ab)
        if valid < tt:
            real = lax.broadcasted_iota(jnp.int32, (tt, LANES), 0) < valid
            log_a = jnp.where(real, log_a, 0.0)
            beta = jnp.where(real, beta, 0.0)
        cum_mat = jnp.where(tri, 1.0, 0.0).astype(BF16)
        la_hi, la_mid, la_lo = _split3(log_a)
        gcum = (_dot(cum_mat, la_hi) + _dot(cum_mat, la_mid)) + _dot(cum_mat, la_lo)
        gcum_s[slot] = gcum
        beta_s[slot] = beta
        env["gcum"], env["gcum_t"], env["beta"] = gcum, gcum.T, beta

    def system_matrix(slot, env, h):
        gcol = env["gcum"][:, h:h + 1]
        grow = env["gcum_t"][h:h + 1, :]
        decay = jnp.exp(jnp.where(tri, gcol - grow, -1e30))
        bcol = env["beta"][:, GDN_HEADS + h:GDN_HEADS + h + 1]
        kb = gk_s[slot, h].astype(BF16)
        r = _dot_nt(jnp.concatenate([gq_s[slot, h].astype(BF16), kb], axis=0), kb)
        a2_s[slot, h, 0:tt, :] = (r[:tt] * decay).astype(BF16)
        l_s[slot, h] = bcol * r[tt:] * jnp.where(strict, decay, 0.0)

    def prework(slot, carry):
        env = {}
        items = [functools.partial(conv_load, slot, carry)]
        items += [functools.partial(conv_piece, slot, p) for p in range(GDN_CONV_CH // LANES)]
        items += [functools.partial(gates, slot, env)]
        items += [functools.partial(system_matrix, slot, env, h) for h in heads]
        return items

    @pl.when(t == 0)
    def _():
        for h in range(RET_HEADS):
            r = (h % 2) * RET_DK
            sret[h] = jnp.zeros((2 * RET_DK, RET_DV), F32)
            sret[h, r:r + RET_DK, :] = rs0_ref[0, h]
        for h in heads:
            sgdn[h] = gs0_ref[0, h]
        xbuf[0:5, :] = jnp.zeros((5, GDN_CONV_CH), F32)
        xbuf[5:8, :] = cv0_ref[0]
        xn0 = _rms_rows(x0_ref[0], nw_ref[...]).astype(BF16)
        for cs in col_blocks:
            proj_s[0, :, cs] = _dot(xn0, win_ref[:, cs])
        for item in prework(0, False):
            item()

    def proj(off, width):
        return proj_s[cur, :, off:off + width]

    def project_next(cs, kc):
        ks = slice(PROJ_K * kc, PROJ_K * (kc + 1))
        part = _dot(xn_s[:, ks], win_ref[ks, cs])
        if kc == 0:
            proj_s[1 - cur, :, cs] = part
        else:
            proj_s[1 - cur, :, cs] += part

    first_half = (lane % RET_DK) < (RET_DK // 2)

    def rope(x):
        rot = jnp.where(first_half, pltpu.roll(x, LANES - RET_DK // 2, 1), pltpu.roll(x, RET_DK // 2, 1))
        return x * cos_ref[...] + rot * sin_ref[...]

    def retention_pair(j):
        qp = rope(proj(OFF_RQ + LANES * j, LANES))
        kp = rope(proj(OFF_RK + LANES * j, LANES)) * (RET_DK ** -0.5)
        qpb = qp.astype(BF16)
        for e in range(2):
            h = 2 * j + e
            km = jnp.where(lane >= RET_DK if e else lane < RET_DK, kp, 0.0)
            v = proj(OFF_RV + RET_DV * h, RET_DV).astype(BF16)
            s_prev = sret[h]
            scores = _dot_nt(qpb, km.astype(BF16)) * dmat_ref[h]
            lhs = jnp.concatenate([scores.astype(BF16), (qp * qdec_ref[h]).astype(BF16)], axis=1)
            o = _dot(lhs, jnp.concatenate([v, s_prev.astype(BF16)], axis=0))
            sret[h] = gam_ref[h] * s_prev + _dot_tn((km * kdec_ref[h]).astype(BF16), v)
            mu = jnp.mean(o, axis=-1, keepdims=True)
            oc = o - mu
            var = jnp.mean(oc * oc, axis=-1, keepdims=True)
            y = oc * lax.rsqrt(var + EPS) * rgn_ref[:, RET_DV * h:RET_DV * (h + 1)]
            cat_ref[0, :, RET_DV * h:RET_DV * (h + 1)] = (y * _silu(proj(OFF_RG + RET_DV * h, RET_DV))).astype(BF16)

    def gmlp_group(g):
        cs = slice(GMLP_DIM * g, GMLP_DIM * (g + 1))
        v = proj(OFF_CV + GMLP_DIM * g, GMLP_DIM)
        mu = jnp.mean(v, axis=-1, keepdims=True)
        vc = v - mu
        var = jnp.mean(vc * vc, axis=-1, keepdims=True)
        vg = vc * lax.rsqrt(var + EPS) * mnw_ref[:, cs]
        if emit_vn:
            vn_ref[0, :, cs] = vg
        wtri = jnp.where(tri, ws_ref[g], 0.0).astype(BF16)
        z = _dot(wtri, vg.astype(BF16)) + bs_ref[g]
        u = proj(OFF_CU + GMLP_DIM * g, GMLP_DIM)
        cat_ref[0, :, RET_V + GDN_V + GMLP_DIM * g:RET_V + GDN_V + GMLP_DIM * (g + 1)] = (u * z).astype(BF16)

    side = [functools.partial(retention_pair, j) for j in range(RET_HEADS // 2)]
    side += [functools.partial(gmlp_group, g) for g in range(GMLP_GROUPS)]
    dense, ahead = [], []
    if pipelined:
        first = [c for c in range(len(col_blocks))
                 if OFF_GQKV <= PROJ_COLS * c < OFF_GQKV + GDN_CONV_CH or PROJ_COLS * c <= OFF_AB < PROJ_COLS * (c + 1)]
        order = first + [c for c in range(len(col_blocks)) if c not in first]
        dense = [functools.partial(project_next, col_blocks[c], kc) for c in order for kc in range(D_MODEL // PROJ_K)]
        n_first = len(first) * (D_MODEL // PROJ_K)
        ahead = prework(1 - cur, True)

    issued = [0]

    def fill(n_dense, n_ahead=0, n_side=0):
        for _ in range(n_dense):
            if dense:
                dense.pop(0)()
                issued[0] += 1
        for _ in range(n_ahead):
            if ahead and issued[0] >= n_first:
                ahead.pop(0)()
        for _ in range(n_side):
            if side:
                side.pop(0)()

    if pipelined:
        xn_s[...] = _rms_rows(x1_ref[0], nw_ref[...]).astype(BF16)

    zero_b = jnp.zeros((tt, tt), BF16)

    def pair_dot(a2, b2):
        bb = b2.astype(BF16)
        rhs = jnp.concatenate([jnp.concatenate([bb[:, :tt], zero_b], axis=1),
                               jnp.concatenate([zero_b, bb[:, tt:]], axis=1)], axis=0)
        return _dot(a2.astype(BF16), rhs)

    def same_block(b):
        return (ri // b) == (ci // b)

    pairs = range(GDN_HEADS // 2)
    eye2 = jnp.concatenate([eye, eye], axis=1)
    l2 = [jnp.concatenate([l_s[cur, 2 * j], l_s[cur, 2 * j + 1]], axis=1) for j in pairs]
    diag = same_block(INV_BLOCK)
    diag2 = jnp.concatenate([diag, diag], axis=1)
    pw = [jnp.where(diag2, -l2[j], 0.0) for j in pairs]
    td = [eye2 + pw[j] for j in pairs]
    n_fac = max(int(math.ceil(math.log2(min(INV_BLOCK, valid)))), 1)
    for it in range(n_fac):
        if it == 0:
            if n_fac > 1:
                pw = [pair_dot(pw[j], pw[j]) for j in pairs]
        elif it == n_fac - 1:
            td = [td[j] + pair_dot(td[j], pw[j]) for j in pairs]
        else:
            r = [pair_dot(jnp.concatenate([pw[j], td[j]], axis=0), pw[j]) for j in pairs]
            pw = [r[j][:tt] for j in pairs]
            td = [td[j] + r[j][tt:] for j in pairs]
        fill(*FILL_SERIES)
    blk = INV_BLOCK
    while blk < valid:
        off = same_block(2 * blk) & jnp.logical_not(same_block(blk))
        off2 = jnp.concatenate([off, off], axis=1)
        y = [pair_dot(jnp.where(off2, l2[j], 0.0), td[j]) for j in pairs]
        fill(*FILL_MERGE)
        z = [pair_dot(td[j], y[j]) for j in pairs]
        td = [td[j] - z[j] for j in pairs]
        fill(*FILL_MERGE)
        blk *= 2

    gcum = gcum_s[cur]
    beta = beta_s[cur]
    gls = []
    for h in heads:
        j, e = divmod(h, 2)
        tinv = td[j][:, tt * e:tt * (e + 1)]
        gcol = gcum[:, h:h + 1]
        eg = jnp.exp(gcol)
        bcol = beta[:, GDN_HEADS + h:GDN_HEADS + h + 1]
        k = gk_s[cur, h]
        rhs = jnp.concatenate([bcol * gv_s[cur, h], (bcol * eg) * k], axis=1).astype(BF16)
        sol = _dot(tinv.astype(BF16), rhs)
        uv_s[h] = sol[:, :GDN_DV]
        a1_s[h, 0:tt, :] = sol[:, GDN_DV:].astype(BF16)
        a1_s[h, tt:2 * tt, :] = (gq_s[cur, h] * eg).astype(BF16)
        glast = gcum[tt - 1:tt, h:h + 1]
        a2_s[cur, h, tt:2 * tt, :] = (k * jnp.exp(glast - gcol)).T.astype(BF16)
        gls.append(jnp.exp(glast))
        fill(*FILL_SOLVE)

    ss = [sgdn[h] for h in heads]
    r1 = [_dot(a1_s[h], ss[h].astype(BF16)) for h in heads]
    fill(*FILL_RECUR)
    ub = [(uv_s[h] - r1[h][:tt]).astype(BF16) for h in heads]
    r2 = [_dot(a2_s[cur, h], ub[h]) for h in heads]
    fill(*FILL_RECUR)
    for h in heads:
        sgdn[h] = ss[h] * gls[h] + r2[h][tt:]
        o = r1[h][tt:] + r2[h][:tt]
        y = o * lax.rsqrt(jnp.mean(o * o, axis=-1, keepdims=True) + EPS) * gnw_ref[...]
        cat_ref[0, :, RET_V + GDN_DV * h:RET_V + GDN_DV * (h + 1)] = (y * _silu(proj(OFF_GZ + GDN_DV * h, GDN_DV))).astype(BF16)

    fill(len(dense), 0, 0)
    fill(0, len(ahead), len(side))

    @pl.when(t == nt - 1)
    def _():
        for h in range(RET_HEADS):
            r = (h % 2) * RET_DK
            rs_ref[0, h] = sret[h, r:r + RET_DK, :]
        for h in heads:
            gs_ref[0, h] = sgdn[h]


def _mixer(x, nw, w_in, l, cosf, sinf, consts, lp, rs0, gs0, cv0, valid, emit_vn):
    b, t, d = x.shape
    tt = TILE
    nt = t // tt
    dmat, qdec, kdec, gam = consts
    full = lambda a: pl.BlockSpec(a.shape, lambda i, j, _n=a.ndim: (0,) * _n)
    small = [cosf, sinf, dmat, qdec, kdec, gam,
             lp["alog"], lp["dtb"], lp["convw"], lp["rgn"], lp["gnw"], lp["mnw"], lp["ws"], lp["bs"]]
    state_specs = [
        pl.BlockSpec((1, RET_HEADS, RET_DK, RET_DV), lambda i, j: (i, 0, 0, 0)),
        pl.BlockSpec((1, GDN_HEADS, GDN_DK, GDN_DV), lambda i, j: (i, 0, 0, 0)),
        pl.BlockSpec((1, CONV_WIDTH - 1, GDN_CONV_CH), lambda i, j: (i, 0, 0)),
    ]
    in_specs = [
        pl.BlockSpec((1, tt, d), lambda i, j: (i, 0, 0)),
        pl.BlockSpec((1, tt, d), lambda i, j: (i, jnp.minimum(j + 1, nt - 1), 0)),
        pl.BlockSpec((1, d), lambda i, j: (0, 0)),
        pl.BlockSpec((None, d, D_PROJ), lambda i, j: (l, 0, 0), pipeline_mode=pl.Buffered(1)),
        pl.BlockSpec((tt, LANES), lambda i, j: (j, 0)),
        pl.BlockSpec((tt, LANES), lambda i, j: (j, 0)),
    ] + [full(a) for a in small[2:]] + state_specs
    out_shape = [
        jax.ShapeDtypeStruct((b, t, D_MIX), BF16),
        jax.ShapeDtypeStruct((b, RET_HEADS, RET_DK, RET_DV), F32),
        jax.ShapeDtypeStruct((b, GDN_HEADS, GDN_DK, GDN_DV), F32),
        jax.ShapeDtypeStruct((b, CONV_WIDTH - 1, GDN_CONV_CH), F32),
    ]
    out_specs = [pl.BlockSpec((1, tt, D_MIX), lambda i, j: (i, j, 0))] + state_specs
    if emit_vn:
        out_shape.append(jax.ShapeDtypeStruct((b, t, GMLP_W), F32))
        out_specs.append(pl.BlockSpec((1, tt, GMLP_W), lambda i, j: (i, j, 0)))
    hs = (GDN_HEADS, tt, LANES)
    scratch = [
        pltpu.VMEM((2, tt, D_PROJ), F32),
        pltpu.VMEM((tt, d), BF16),
        pltpu.VMEM((RET_HEADS, 2 * RET_DK, RET_DV), F32),
        pltpu.VMEM((GDN_HEADS, GDN_DK, GDN_DV), F32),
        pltpu.VMEM((tt + 8, GDN_CONV_CH), F32),
        pltpu.VMEM((2,) + hs, F32), pltpu.VMEM((2,) + hs, F32), pltpu.VMEM((2,) + hs, F32),
        pltpu.VMEM((2, GDN_HEADS, tt, tt), F32),
        pltpu.VMEM((2, tt, LANES), F32), pltpu.VMEM((2, tt, LANES), F32),
        pltpu.VMEM(hs, F32),
        pltpu.VMEM((GDN_HEADS, 2 * tt, LANES), BF16),
        pltpu.VMEM((2, GDN_HEADS, 2 * tt, LANES), BF16),
    ]
    return pl.pallas_call(
        functools.partial(_mixer_kernel, valid, emit_vn, nt > 1),
        grid=(b, nt),
        in_specs=in_specs,
        out_specs=out_specs,
        out_shape=out_shape,
        scratch_shapes=scratch,
        compiler_params=_cparams(("parallel", "arbitrary")),
        name="mixer",
    )(x, x, nw, w_in, *small, rs0, gs0, cv0)


def _pad_rows(a, rows, axis):
    pad = [(0, 0)] * a.ndim
    pad[axis] = (0, rows - a.shape[axis])
    return jnp.pad(a, pad)


def _rope_tables(pos):
    half = RET_DK // 2
    inv = jnp.power(ROPE_BASE, -jnp.arange(half, dtype=F32) / half)
    ang = pos.astype(F32)[:, None] * inv[None, :]
    cos, sin = jnp.cos(ang), jnp.sin(ang)
    reps = LANES // RET_DK
    return jnp.tile(jnp.concatenate([cos, cos], axis=1), (1, reps)), jnp.tile(jnp.concatenate([-sin, sin], axis=1), (1, reps))


def _ret_consts(cr):
    log_gamma = jnp.log1p(-jnp.power(2.0, -5.0 - jnp.arange(RET_HEADS, dtype=F32)))
    idx = jnp.arange(cr, dtype=F32)
    diff = idx[:, None] - idx[None, :]
    causal = diff >= 0
    dmat = jnp.where(causal[None], jnp.exp(jnp.where(causal, diff, 0.0)[None] * log_gamma[:, None, None]), 0.0)
    kdec = jnp.exp(log_gamma[:, None] * (cr - 1 - idx)[None, :])[:, :, None]
    qdec = jnp.exp(log_gamma[:, None] * (idx + 1.0)[None, :])[:, :, None]
    gam = jnp.broadcast_to(jnp.exp(cr * log_gamma)[:, None, None], (RET_HEADS, 1, LANES))
    dmat = _pad_rows(_pad_rows(dmat, TILE, 1), TILE, 2)
    return dmat, _pad_rows(qdec, TILE, 1), _pad_rows(kdec, TILE, 1), gam


def _pad_lanes(v):
    return jnp.pad(v, (0, LANES - v.shape[0]))[None, :]


def _tiles(n):
    return dict(tm_ffn=min(1024, n), tf=512, tm_out=min(512, n))


def _trunk(x, pos, rs0, gs0, cv0, p, emit_vn):
    b, t, d = x.shape
    n = b * t
    tl = _tiles(n)
    valid = min(t, TILE)
    tpad = -(-t // TILE) * TILE
    cosf, sinf = _rope_tables(pos)
    cosf, sinf = _pad_rows(cosf, tpad, 0), _pad_rows(sinf, tpad, 0)
    consts = _ret_consts(valid)
    x = x.reshape(n, d)
    outs = []
    for l in range(DEPTH):
        x = _ffn(x, p["norm_ffn1_w"][l][None], p["ffn1_w_up"], p["ffn1_w_down"], l, tl["tm_ffn"], tl["tf"])
        lp = {
            "alog": _pad_lanes(p["gdn_a_log"][l]), "dtb": _pad_lanes(p["gdn_dt_bias"][l]),
            "convw": p["gdn_conv_w"][l], "rgn": p["ret_gn_w"][l][None], "gnw": p["gdn_norm_w"][l][None],
            "mnw": p["gmlp_norm_w"][l][None],
            "ws": _pad_rows(_pad_rows(p["gmlp_ws"][l][:, :valid, :valid], TILE, 1), TILE, 2),
            "bs": _pad_rows(p["gmlp_bs"][l][:, :valid, None], TILE, 1),
        }
        res = _mixer(_pad_rows(x.reshape(b, t, d), tpad, 1), p["norm_mix_w"][l][None], p["w_in"], l,
                     cosf, sinf, consts, lp, rs0[l], gs0[l], cv0[l], valid, emit_vn)
        cat = res[0][:, :t].reshape(n, D_MIX)
        x = _outproj(x, cat, p["w_out"], l, tl["tm_out"])
        x = _ffn(x, p["norm_ffn2_w"][l][None], p["ffn2_w_up"], p["ffn2_w_down"], l, tl["tm_ffn"], tl["tf"],
                 final_nw=p["final_norm_w"][None] if l == DEPTH - 1 else None)
        outs.append(tuple(res[1:4]) + ((res[4][:, :t],) if emit_vn else ()))
    y = x.reshape(b, t, d)
    return (y,) + tuple(jnp.stack([o[i] for o in outs]) for i in range(len(outs[0])))


def _win_kernel(w_ref, o_ref):
    ab0 = OFF_CU
    nab = 2 * GDN_HEADS
    o_ref[:, :ab0] = w_ref[:, :ab0].astype(BF16)
    o_ref[:, OFF_CU:OFF_AB] = w_ref[:, ab0 + nab:].astype(BF16)
    o_ref[:, OFF_AB:] = jnp.zeros((o_ref.shape[0], D_PROJ - OFF_AB), BF16)
    o_ref[:, OFF_AB:OFF_AB + nab] = w_ref[:, ab0:ab0 + nab].astype(BF16)


def _relayout_w_in(w, tr=256):
    depth, d, d_in = w.shape
    return pl.pallas_call(
        _win_kernel,
        grid=(depth, d // tr),
        in_specs=[pl.BlockSpec((None, tr, d_in), lambda l, i: (l, i, 0))],
        out_specs=pl.BlockSpec((None, tr, D_PROJ), lambda l, i: (l, i, 0)),
        out_shape=jax.ShapeDtypeStruct((depth, d, D_PROJ), BF16),
        compiler_params=_cparams(("parallel", "parallel")),
        name="w_in_layout",
    )(w)


def _prep_params(p):
    q = dict(p)
    for k in ("ffn1_w_up", "ffn1_w_down", "ffn2_w_up", "ffn2_w_down", "w_out"):
        q[k] = p[k].astype(BF16)
    q["w_in"] = _relayout_w_in(p["w_in"])
    return q


def kernel(x_prompt, x_sample, state_ret, state_gdn, state_conv, norm_ffn1_w, ffn1_w_up, ffn1_w_down,
           norm_mix_w, w_in, ret_gn_w, gdn_conv_w, gdn_a_log, gdn_dt_bias, gdn_norm_w, gmlp_ws, gmlp_bs,
           gmlp_norm_w, w_out, norm_ffn2_w, ffn2_w_up, ffn2_w_down, final_norm_w):
    p = _prep_params({
        "norm_ffn1_w": norm_ffn1_w, "ffn1_w_up": ffn1_w_up, "ffn1_w_down": ffn1_w_down,
        "norm_mix_w": norm_mix_w, "w_in": w_in, "ret_gn_w": ret_gn_w, "gdn_conv_w": gdn_conv_w,
        "gdn_a_log": gdn_a_log, "gdn_dt_bias": gdn_dt_bias, "gdn_norm_w": gdn_norm_w,
        "gmlp_ws": gmlp_ws, "gmlp_bs": gmlp_bs, "gmlp_norm_w": gmlp_norm_w, "w_out": w_out,
        "norm_ffn2_w": norm_ffn2_w, "ffn2_w_up": ffn2_w_up, "ffn2_w_down": ffn2_w_down,
        "final_norm_w": final_norm_w})
    bp, tp = x_prompt.shape[0], x_prompt.shape[1]
    bs, ts = x_sample.shape[0], x_sample.shape[1]
    ret0 = jnp.zeros((DEPTH, bp, RET_HEADS, RET_DK, RET_DV), F32)
    gdn0 = jnp.zeros((DEPTH, bp, GDN_HEADS, GDN_DK, GDN_DV), F32)
    conv0 = jnp.zeros((DEPTH, bp, CONV_WIDTH - 1, GDN_CONV_CH), F32)
    pos_p = jnp.arange(tp, dtype=jnp.int32)
    y_p, ret_p, gdn_p, conv_p = _trunk(x_prompt, pos_p, ret0, gdn0, conv0, p, emit_vn=False)
    pos_s = PAST_LEN + jnp.arange(ts, dtype=jnp.int32)
    y_s, ret_s, gdn_s, conv_s, vn_s = _trunk(x_sample, pos_s, state_ret, state_gdn, state_conv, p, emit_vn=True)
    return (y_p, y_s, ret_p, ret_s, gdn_p, gdn_s, conv_p, conv_s, vn_s)
```

```python
import functools
import math

import jax
import jax.numpy as jnp
from jax import lax
from jax.experimental import pallas as pl
from jax.experimental.pallas import tpu as pltpu

F32 = jnp.float32
BF16 = jnp.bfloat16

D_MODEL = 2048
DEPTH = 4
PAST_LEN = 4096
EPS = 1e-6
ROPE_BASE = 10000.0
RET_HEADS = 6
RET_DK = 64
RET_DV = 128
GDN_HEADS = 6
GDN_DK = 128
GDN_DV = 128
CONV_WIDTH = 4
GMLP_GROUPS = 4
GMLP_DIM = 128
GMLP_CHUNK = 128
D_FF = 5632

RET_QK = RET_HEADS * RET_DK
RET_V = RET_HEADS * RET_DV
GDN_QK = GDN_HEADS * GDN_DK
GDN_V = GDN_HEADS * GDN_DV
GDN_CONV_CH = 2 * GDN_QK + GDN_V
GMLP_W = GMLP_GROUPS * GMLP_DIM
D_MIX = RET_V + GDN_V + GMLP_W
D_IN = RET_QK * 2 + RET_V * 2 + GDN_CONV_CH + GDN_V + GDN_HEADS * 2 + GMLP_W * 2

LANES = 128
TILE = 128
INV_BLOCK = 16
FFN_TF = 512
PROJ_COLS = 256
PROJ_K = 2048
FILL_SERIES, FILL_MERGE, FILL_SOLVE, FILL_RECUR = (3, 0, 1), (2, 3, 1), (0, 1, 0), (1, 1, 0)

OFF_RQ = 0
OFF_RK = OFF_RQ + RET_QK
OFF_RV = OFF_RK + RET_QK
OFF_RG = OFF_RV + RET_V
OFF_GQKV = OFF_RG + RET_V
OFF_GZ = OFF_GQKV + GDN_CONV_CH
OFF_CU = OFF_GZ + GDN_V
OFF_CV = OFF_CU + GMLP_W
OFF_AB = OFF_CV + GMLP_W
D_PROJ = 6656

VMEM_LIMIT = 56 * 1024 * 1024


def _cparams(sem):
    return pltpu.CompilerParams(dimension_semantics=sem, vmem_limit_bytes=VMEM_LIMIT)


def _dot(a, b):
    return jnp.dot(a, b, preferred_element_type=F32)


def _dot_nt(a, b):
    return lax.dot_general(a, b, (((1,), (1,)), ((), ())), preferred_element_type=F32)


def _dot_tn(a, b):
    return lax.dot_general(a, b, (((0,), (0,)), ((), ())), preferred_element_type=F32)


def _silu(x):
    return x * jax.nn.sigmoid(x)


def _rms_rows(x, w):
    return x * lax.rsqrt(jnp.mean(x * x, axis=-1, keepdims=True) + EPS) * w


def _ffn_kernel(x_ref, nw_ref, wg_ref, wu_ref, wd_ref, *rest):
    fnw_ref = rest[0] if len(rest) == 3 else None
    o_ref, xn_ref = rest[-2:]
    j = pl.program_id(1)

    def delta():
        xn = xn_ref[...]
        g = _dot(xn, wg_ref[...])
        u = _dot(xn, wu_ref[...])
        a = (0.5 * _silu(g) * u).astype(BF16)
        return _dot(a, wd_ref[...])

    @pl.when(j == 0)
    def _():
        xn_ref[...] = _rms_rows(x_ref[...], nw_ref[...]).astype(BF16)
        o_ref[...] = x_ref[...] + delta()

    @pl.when(j != 0)
    def _():
        o_ref[...] += delta()

    if fnw_ref is not None:
        @pl.when(j == pl.num_programs(1) - 1)
        def _():
            o_ref[...] = _rms_rows(o_ref[...], fnw_ref[...])


def _ffn(x, nw, w_up, w_down, l, tm, final_nw=None):
    n, d = x.shape
    nf, tf = w_up.shape[1] // 2, w_up.shape[3]
    vec = pl.BlockSpec((1, d), lambda i, j: (0, 0))
    return pl.pallas_call(
        _ffn_kernel,
        grid=(n // tm, nf),
        in_specs=[
            pl.BlockSpec((tm, d), lambda i, j: (i, 0)),
            vec,
            pl.BlockSpec((None, None, d, tf), lambda i, j: (l, j, 0, 0)),
            pl.BlockSpec((None, None, d, tf), lambda i, j: (l, j + nf, 0, 0)),
            pl.BlockSpec((None, tf, d), lambda i, j: (l, j, 0)),
        ] + ([] if final_nw is None else [vec]),
        out_specs=pl.BlockSpec((tm, d), lambda i, j: (i, 0)),
        out_shape=jax.ShapeDtypeStruct((n, d), F32),
        scratch_shapes=[pltpu.VMEM((tm, d), BF16)],
        compiler_params=_cparams(("parallel", "arbitrary")),
        name="ffn",
    )(x, nw, w_up, w_up, w_down, *(() if final_nw is None else (final_nw,)))


def _outproj_kernel(x_ref, c_ref, w_ref, o_ref):
    o_ref[...] = x_ref[...] + _dot(c_ref[...], w_ref[...])


def _outproj(x, cat, w, l, tm):
    n, d = x.shape
    k = cat.shape[1]
    return pl.pallas_call(
        _outproj_kernel,
        grid=(n // tm,),
        in_specs=[
            pl.BlockSpec((tm, d), lambda i: (i, 0)),
            pl.BlockSpec((tm, k), lambda i: (i, 0)),
            pl.BlockSpec((None, k, d), lambda i: (l, 0, 0)),
        ],
        out_specs=pl.BlockSpec((tm, d), lambda i: (i, 0)),
        out_shape=jax.ShapeDtypeStruct((n, d), F32),
        compiler_params=_cparams(("parallel",)),
        name="outproj",
    )(x, cat, w)


def _split3(x):
    hi = x.astype(BF16)
    r1 = x - hi.astype(F32)
    mid = r1.astype(BF16)
    lo = (r1 - mid.astype(F32)).astype(BF16)
    return hi, mid, lo


def _mixer_kernel(valid, emit_vn, pipelined,
                  x0_ref, x1_ref, nw_ref, win_ref,
                  cos_ref, sin_ref, dmat_ref, qdec_ref, kdec_ref, gam_ref,
                  alog_ref, dtb_ref, convw_ref, rgn_ref, gnw_ref, mnw_ref, ws_ref, bs_ref,
                  rs0_ref, gs0_ref, cv0_ref,
                  cat_ref, rs_ref, gs_ref, cvn_ref, *rest):
    if emit_vn:
        vn_ref = rest[0]
        rest = rest[1:]
    else:
        vn_ref = None
    proj_s, xn_s, sret, sgdn, xbuf, gq_s, gk_s, gv_s, l_s, gcum_s, beta_s, uv_s, a1_s, a2_s = rest
    tt = TILE
    t = pl.program_id(1)
    nt = pl.num_programs(1)
    cur = t % 2
    heads = range(GDN_HEADS)
    col_blocks = [slice(PROJ_COLS * c, PROJ_COLS * (c + 1)) for c in range(D_PROJ // PROJ_COLS)]

    lane = lax.broadcasted_iota(jnp.int32, (tt, LANES), 1)
    ri = lax.broadcasted_iota(jnp.int32, (tt, tt), 0)
    ci = lax.broadcasted_iota(jnp.int32, (tt, tt), 1)
    tri = ri >= ci
    strict = ri > ci
    eye = jnp.where(ri == ci, 1.0, 0.0).astype(F32)

    def conv_load(slot, carry):
        if carry:
            xbuf[5:8, :] = xbuf[5 + tt:8 + tt, :]
        xbuf[8:8 + tt, :] = proj_s[slot, :, OFF_GQKV:OFF_GQKV + GDN_CONV_CH]
        cvn_ref[0] = xbuf[5 + valid:8 + valid, :]

    def conv_piece(slot, p):
        cs = slice(LANES * p, LANES * (p + 1))
        xc = xbuf[:, cs]
        y = xc * convw_ref[0:1, cs]
        for i in range(1, CONV_WIDTH):
            y = pltpu.roll(y, 1, 0) + xc * convw_ref[i:i + 1, cs]
        y = _silu(y[8:8 + tt])
        kind, h = divmod(p, GDN_HEADS)
        if kind == 2:
            gv_s[slot, h] = y
        else:
            yn = y * lax.rsqrt(jnp.sum(y * y, axis=-1, keepdims=True) + EPS)
            if kind == 0:
                gq_s[slot, h] = yn * (GDN_DK ** -0.5)
            else:
                gk_s[slot, h] = yn

    def gates(slot, env):
        ab = proj_s[slot, :, OFF_AB:OFF_AB + LANES]
        log_a = -jnp.exp(alog_ref[...]) * jax.nn.softplus(ab + dtb_ref[...])
        beta = jax.nn.sigmoid(ab)
        if valid < tt:
            real = lax.broadcasted_iota(jnp.int32, (tt, LANES), 0) < valid
            log_a = jnp.where(real, log_a, 0.0)
            beta = jnp.where(real, beta, 0.0)
        cum_mat = jnp.where(tri, 1.0, 0.0).astype(BF16)
        la_hi, la_mid, la_lo = _split3(log_a)
        gcum = (_dot(cum_mat, la_hi) + _dot(cum_mat, la_mid)) + _dot(cum_mat, la_lo)
        gcum_s[slot] = gcum
        beta_s[slot] = beta
        env["gcum"], env["gcum_t"], env["beta"] = gcum, gcum.T, beta

    def system_matrix(slot, env, h):
        gcol = env["gcum"][:, h:h + 1]
        grow = env["gcum_t"][h:h + 1, :]
        decay = jnp.exp(jnp.where(tri, gcol - grow, -1e30))
        bcol = env["beta"][:, GDN_HEADS + h:GDN_HEADS + h + 1]
        kb = gk_s[slot, h].astype(BF16)
        r = _dot_nt(jnp.concatenate([gq_s[slot, h].astype(BF16), kb], axis=0), kb)
        a2_s[slot, h, 0:tt, :] = (r[:tt] * decay).astype(BF16)
        l_s[slot, h] = bcol * r[tt:] * jnp.where(strict, decay, 0.0)

    def prework(slot, carry):
        env = {}
        items = [functools.partial(conv_load, slot, carry)]
        items += [functools.partial(conv_piece, slot, p) for p in range(GDN_CONV_CH // LANES)]
        items += [functools.partial(gates, slot, env)]
        items += [functools.partial(system_matrix, slot, env, h) for h in heads]
        return items

    @pl.when(t == 0)
    def _():
        for h in range(RET_HEADS):
            r = (h % 2) * RET_DK
            sret[h] = jnp.zeros((2 * RET_DK, RET_DV), F32)
            sret[h, r:r + RET_DK, :] = rs0_ref[0, h]
        for h in heads:
            sgdn[h] = gs0_ref[0, h]
        xbuf[0:5, :] = jnp.zeros((5, GDN_CONV_CH), F32)
        xbuf[5:8, :] = cv0_ref[0]
        xn0 = _rms_rows(x0_ref[0], nw_ref[...]).astype(BF16)
        for cs in col_blocks:
            proj_s[0, :, cs] = _dot(xn0, win_ref[:, cs])
        for item in prework(0, False):
            item()

    def proj(off, width):
        return proj_s[cur, :, off:off + width]

    def project_next(cs, kc):
        ks = slice(PROJ_K * kc, PROJ_K * (kc + 1))
        part = _dot(xn_s[:, ks], win_ref[ks, cs])
        if kc == 0:
            proj_s[1 - cur, :, cs] = part
        else:
            proj_s[1 - cur, :, cs] += part

    first_half = (lane % RET_DK) < (RET_DK // 2)

    def rope(x):
        rot = jnp.where(first_half, pltpu.roll(x, LANES - RET_DK // 2, 1), pltpu.roll(x, RET_DK // 2, 1))
        return x * cos_ref[...] + rot * sin_ref[...]

    def retention_pair(j):
        qp = rope(proj(OFF_RQ + LANES * j, LANES))
        kp = rope(proj(OFF_RK + LANES * j, LANES)) * (RET_DK ** -0.5)
        qpb = qp.astype(BF16)
        for e in range(2):
            h = 2 * j + e
            km = jnp.where(lane >= RET_DK if e else lane < RET_DK, kp, 0.0)
            v = proj(OFF_RV + RET_DV * h, RET_DV).astype(BF16)
            s_prev = sret[h]
            scores = _dot_nt(qpb, km.astype(BF16)) * dmat_ref[h]
            lhs = jnp.concatenate([scores.astype(BF16), (qp * qdec_ref[h]).astype(BF16)], axis=1)
            o = _dot(lhs, jnp.concatenate([v, s_prev.astype(BF16)], axis=0))
            sret[h] = gam_ref[h] * s_prev + _dot_tn((km * kdec_ref[h]).astype(BF16), v)
            mu = jnp.mean(o, axis=-1, keepdims=True)
            oc = o - mu
            var = jnp.mean(oc * oc, axis=-1, keepdims=True)
            y = oc * lax.rsqrt(var + EPS) * rgn_ref[:, RET_DV * h:RET_DV * (h + 1)]
            cat_ref[0, :, RET_DV * h:RET_DV * (h + 1)] = (y * _silu(proj(OFF_RG + RET_DV * h, RET_DV))).astype(BF16)

    def gmlp_group(g):
        cs = slice(GMLP_DIM * g, GMLP_DIM * (g + 1))
        v = proj(OFF_CV + GMLP_DIM * g, GMLP_DIM)
        mu = jnp.mean(v, axis=-1, keepdims=True)
        vc = v - mu
        var = jnp.mean(vc * vc, axis=-1, keepdims=True)
        vg = vc * lax.rsqrt(var + EPS) * mnw_ref[:, cs]
        if emit_vn:
            vn_ref[0, :, cs] = vg
        wtri = jnp.where(tri, ws_ref[g], 0.0).astype(BF16)
        z = _dot(wtri, vg.astype(BF16)) + bs_ref[g]
        u = proj(OFF_CU + GMLP_DIM * g, GMLP_DIM)
        cat_ref[0, :, RET_V + GDN_V + GMLP_DIM * g:RET_V + GDN_V + GMLP_DIM * (g + 1)] = (u * z).astype(BF16)

    side = [functools.partial(retention_pair, j) for j in range(RET_HEADS // 2)]
    side += [functools.partial(gmlp_group, g) for g in range(GMLP_GROUPS)]
    dense, ahead = [], []
    if pipelined:
        first = [c for c in range(len(col_blocks))
                 if OFF_GQKV <= PROJ_COLS * c < OFF_GQKV + GDN_CONV_CH or PROJ_COLS * c <= OFF_AB < PROJ_COLS * (c + 1)]
        order = first + [c for c in range(len(col_blocks)) if c not in first]
        dense = [functools.partial(project_next, col_blocks[c], kc) for c in order for kc in range(D_MODEL // PROJ_K)]
        n_first = len(first) * (D_MODEL // PROJ_K)
        ahead = prework(1 - cur, True)

    issued = [0]

    def fill(n_dense, n_ahead=0, n_side=0):
        for _ in range(n_dense):
            if dense:
                dense.pop(0)()
                issued[0] += 1
        for _ in range(n_ahead):
            if ahead and issued[0] >= n_first:
                ahead.pop(0)()
        for _ in range(n_side):
            if side:
                side.pop(0)()

    if pipelined:
        xn_s[...] = _rms_rows(x1_ref[0], nw_ref[...]).astype(BF16)

    zero_b = jnp.zeros((tt, tt), BF16)

    def pair_dot(a2, b2):
        bb = b2.astype(BF16)
        rhs = jnp.concatenate([jnp.concatenate([bb[:, :tt], zero_b], axis=1),
                               jnp.concatenate([zero_b, bb[:, tt:]], axis=1)], axis=0)
        return _dot(a2.astype(BF16), rhs)

    def same_block(b):
        return (ri // b) == (ci // b)

    pairs = range(GDN_HEADS // 2)
    eye2 = jnp.concatenate([eye, eye], axis=1)
    l2 = [jnp.concatenate([l_s[cur, 2 * j], l_s[cur, 2 * j + 1]], axis=1) for j in pairs]
    diag = same_block(INV_BLOCK)
    diag2 = jnp.concatenate([diag, diag], axis=1)
    pw = [jnp.where(diag2, -l2[j], 0.0) for j in pairs]
    td = [eye2 + pw[j] for j in pairs]
    n_fac = max(int(math.ceil(math.log2(min(INV_BLOCK, valid)))), 1)
    for it in range(n_fac):
        if it == 0:
            if n_fac > 1:
                pw = [pair_dot(pw[j], pw[j]) for j in pairs]
        elif it == n_fac - 1:
            td = [td[j] + pair_dot(td[j], pw[j]) for j in pairs]
        else:
            r = [pair_dot(jnp.concatenate([pw[j], td[j]], axis=0), pw[j]) for j in pairs]
            pw = [r[j][:tt] for j in pairs]
            td = [td[j] + r[j][tt:] for j in pairs]
        fill(*FILL_SERIES)
    blk = INV_BLOCK
    while blk < valid:
        off = same_block(2 * blk) & jnp.logical_not(same_block(blk))
        off2 = jnp.concatenate([off, off], axis=1)
        y = [pair_dot(jnp.where(off2, l2[j], 0.0), td[j]) for j in pairs]
        fill(*FILL_MERGE)
        z = [pair_dot(td[j], y[j]) for j in pairs]
        td = [td[j] - z[j] for j in pairs]
        fill(*FILL_MERGE)
        blk *= 2

    gcum = gcum_s[cur]
    beta = beta_s[cur]
    gls = []
    for h in heads:
        j, e = divmod(h, 2)
        tinv = td[j][:, tt * e:tt * (e + 1)]
        gcol = gcum[:, h:h + 1]
        eg = jnp.exp(gcol)
        bcol = beta[:, GDN_HEADS + h:GDN_HEADS + h + 1]
        k = gk_s[cur, h]
        rhs = jnp.concatenate([bcol * gv_s[cur, h], (bcol * eg) * k], axis=1).astype(BF16)
        sol = _dot(tinv.astype(BF16), rhs)
        uv_s[h] = sol[:, :GDN_DV]
        a1_s[h, 0:tt, :] = sol[:, GDN_DV:].astype(BF16)
        a1_s[h, tt:2 * tt, :] = (gq_s[cur, h] * eg).astype(BF16)
        glast = gcum[tt - 1:tt, h:h + 1]
        a2_s[cur, h, tt:2 * tt, :] = (k * jnp.exp(glast - gcol)).T.astype(BF16)
        gls.append(jnp.exp(glast))
        fill(*FILL_SOLVE)

    ss = [sgdn[h] for h in heads]
    r1 = [_dot(a1_s[h], ss[h].astype(BF16)) for h in heads]
    fill(*FILL_RECUR)
    ub = [(uv_s[h] - r1[h][:tt]).astype(BF16) for h in heads]
    r2 = [_dot(a2_s[cur, h], ub[h]) for h in heads]
    fill(*FILL_RECUR)
    for h in heads:
        sgdn[h] = ss[h] * gls[h] + r2[h][tt:]
        o = r1[h][tt:] + r2[h][:tt]
        y = o * lax.rsqrt(jnp.mean(o * o, axis=-1, keepdims=True) + EPS) * gnw_ref[...]
        cat_ref[0, :, RET_V + GDN_DV * h:RET_V + GDN_DV * (h + 1)] = (y * _silu(proj(OFF_GZ + GDN_DV * h, GDN_DV))).astype(BF16)

    fill(len(dense), 0, 0)
    fill(0, len(ahead), len(side))

    @pl.when(t == nt - 1)
    def _():
        for h in range(RET_HEADS):
            r = (h % 2) * RET_DK
            rs_ref[0, h] = sret[h, r:r + RET_DK, :]
        for h in heads:
            gs_ref[0, h] = sgdn[h]


def _mixer(x, nw, w_in, l, cosf, sinf, consts, lp, rs0, gs0, cv0, valid, emit_vn):
    b, t, d = x.shape
    tt = TILE
    nt = t // tt
    dmat, qdec, kdec, gam = consts
    full = lambda a: pl.BlockSpec(a.shape, lambda i, j, _n=a.ndim: (0,) * _n)
    small = [cosf, sinf, dmat, qdec, kdec, gam,
             lp["alog"], lp["dtb"], lp["convw"], lp["rgn"], lp["gnw"], lp["mnw"], lp["ws"], lp["bs"]]
    state_specs = [
        pl.BlockSpec((1, RET_HEADS, RET_DK, RET_DV), lambda i, j: (i, 0, 0, 0)),
        pl.BlockSpec((1, GDN_HEADS, GDN_DK, GDN_DV), lambda i, j: (i, 0, 0, 0)),
        pl.BlockSpec((1, CONV_WIDTH - 1, GDN_CONV_CH), lambda i, j: (i, 0, 0)),
    ]
    in_specs = [
        pl.BlockSpec((1, tt, d), lambda i, j: (i, 0, 0)),
        pl.BlockSpec((1, tt, d), lambda i, j: (i, jnp.minimum(j + 1, nt - 1), 0)),
        pl.BlockSpec((1, d), lambda i, j: (0, 0)),
        pl.BlockSpec((None, d, D_PROJ), lambda i, j: (l, 0, 0), pipeline_mode=pl.Buffered(1)),
        pl.BlockSpec((tt, LANES), lambda i, j: (j, 0)),
        pl.BlockSpec((tt, LANES), lambda i, j: (j, 0)),
    ] + [full(a) for a in small[2:]] + state_specs
    out_shape = [
        jax.ShapeDtypeStruct((b, t, D_MIX), BF16),
        jax.ShapeDtypeStruct((b, RET_HEADS, RET_DK, RET_DV), F32),
        jax.ShapeDtypeStruct((b, GDN_HEADS, GDN_DK, GDN_DV), F32),
        jax.ShapeDtypeStruct((b, CONV_WIDTH - 1, GDN_CONV_CH), F32),
    ]
    out_specs = [pl.BlockSpec((1, tt, D_MIX), lambda i, j: (i, j, 0))] + state_specs
    if emit_vn:
        out_shape.append(jax.ShapeDtypeStruct((b, t, GMLP_W), F32))
        out_specs.append(pl.BlockSpec((1, tt, GMLP_W), lambda i, j: (i, j, 0)))
    hs = (GDN_HEADS, tt, LANES)
    scratch = [
        pltpu.VMEM((2, tt, D_PROJ), F32),
        pltpu.VMEM((tt, d), BF16),
        pltpu.VMEM((RET_HEADS, 2 * RET_DK, RET_DV), F32),
        pltpu.VMEM((GDN_HEADS, GDN_DK, GDN_DV), F32),
        pltpu.VMEM((tt + 8, GDN_CONV_CH), F32),
        pltpu.VMEM((2,) + hs, F32), pltpu.VMEM((2,) + hs, F32), pltpu.VMEM((2,) + hs, F32),
        pltpu.VMEM((2, GDN_HEADS, tt, tt), F32),
        pltpu.VMEM((2, tt, LANES), F32), pltpu.VMEM((2, tt, LANES), F32),
        pltpu.VMEM(hs, F32),
        pltpu.VMEM((GDN_HEADS, 2 * tt, LANES), BF16),
        pltpu.VMEM((2, GDN_HEADS, 2 * tt, LANES), BF16),
    ]
    return pl.pallas_call(
        functools.partial(_mixer_kernel, valid, emit_vn, nt > 1),
        grid=(b, nt),
        in_specs=in_specs,
        out_specs=out_specs,
        out_shape=out_shape,
        scratch_shapes=scratch,
        compiler_params=_cparams(("parallel", "arbitrary")),
        name="mixer",
    )(x, x, nw, w_in, *small, rs0, gs0, cv0)


def _pad_rows(a, rows, axis):
    pad = [(0, 0)] * a.ndim
    pad[axis] = (0, rows - a.shape[axis])
    return jnp.pad(a, pad)


def _rope_tables(pos):
    half = RET_DK // 2
    inv = jnp.power(ROPE_BASE, -jnp.arange(half, dtype=F32) / half)
    ang = pos.astype(F32)[:, None] * inv[None, :]
    cos, sin = jnp.cos(ang), jnp.sin(ang)
    reps = LANES // RET_DK
    return jnp.tile(jnp.concatenate([cos, cos], axis=1), (1, reps)), jnp.tile(jnp.concatenate([-sin, sin], axis=1), (1, reps))


def _ret_consts(cr):
    log_gamma = jnp.log1p(-jnp.power(2.0, -5.0 - jnp.arange(RET_HEADS, dtype=F32)))
    idx = jnp.arange(cr, dtype=F32)
    diff = idx[:, None] - idx[None, :]
    causal = diff >= 0
    dmat = jnp.where(causal[None], jnp.exp(jnp.where(causal, diff, 0.0)[None] * log_gamma[:, None, None]), 0.0)
    kdec = jnp.exp(log_gamma[:, None] * (cr - 1 - idx)[None, :])[:, :, None]
    qdec = jnp.exp(log_gamma[:, None] * (idx + 1.0)[None, :])[:, :, None]
    gam = jnp.broadcast_to(jnp.exp(cr * log_gamma)[:, None, None], (RET_HEADS, 1, LANES))
    dmat = _pad_rows(_pad_rows(dmat, TILE, 1), TILE, 2)
    return dmat, _pad_rows(qdec, TILE, 1), _pad_rows(kdec, TILE, 1), gam


def _pad_lanes(v):
    return jnp.pad(v, (0, LANES - v.shape[0]))[None, :]


def _tiles(n):
    return dict(tm_ffn=min(1024, n), tm_out=min(512, n))


def _trunk(x, pos, rs0, gs0, cv0, p, emit_vn):
    b, t, d = x.shape
    n = b * t
    tl = _tiles(n)
    valid = min(t, TILE)
    tpad = -(-t // TILE) * TILE
    cosf, sinf = _rope_tables(pos)
    cosf, sinf = _pad_rows(cosf, tpad, 0), _pad_rows(sinf, tpad, 0)
    consts = _ret_consts(valid)
    x = x.reshape(n, d)
    outs = []
    for l in range(DEPTH):
        x = _ffn(x, p["norm_ffn1_w"][l][None], p["ffn1_w_up"], p["ffn1_w_down"], l, tl["tm_ffn"])
        lp = {
            "alog": _pad_lanes(p["gdn_a_log"][l]), "dtb": _pad_lanes(p["gdn_dt_bias"][l]),
            "convw": p["gdn_conv_w"][l], "rgn": p["ret_gn_w"][l][None], "gnw": p["gdn_norm_w"][l][None],
            "mnw": p["gmlp_norm_w"][l][None],
            "ws": _pad_rows(_pad_rows(p["gmlp_ws"][l][:, :valid, :valid], TILE, 1), TILE, 2),
            "bs": _pad_rows(p["gmlp_bs"][l][:, :valid, None], TILE, 1),
        }
        res = _mixer(_pad_rows(x.reshape(b, t, d), tpad, 1), p["norm_mix_w"][l][None], p["w_in"], l,
                     cosf, sinf, consts, lp, rs0[l], gs0[l], cv0[l], valid, emit_vn)
        cat = res[0][:, :t].reshape(n, D_MIX)
        x = _outproj(x, cat, p["w_out"], l, tl["tm_out"])
        x = _ffn(x, p["norm_ffn2_w"][l][None], p["ffn2_w_up"], p["ffn2_w_down"], l, tl["tm_ffn"],
                 final_nw=p["final_norm_w"][None] if l == DEPTH - 1 else None)
        outs.append(tuple(res[1:4]) + ((res[4][:, :t],) if emit_vn else ()))
    y = x.reshape(b, t, d)
    return (y,) + tuple(jnp.stack([o[i] for o in outs]) for i in range(len(outs[0])))


def _win_kernel(w_ref, o_ref):
    ab0 = OFF_CU
    nab = 2 * GDN_HEADS
    o_ref[:, :ab0] = w_ref[:, :ab0]
    o_ref[:, OFF_CU:OFF_AB] = w_ref[:, ab0 + nab:]
    o_ref[:, OFF_AB:] = jnp.zeros((o_ref.shape[0], D_PROJ - OFF_AB), BF16)
    o_ref[:, OFF_AB:OFF_AB + nab] = w_ref[:, ab0:ab0 + nab]


def _relayout_w_in(w, tr=256):
    depth, d, d_in = w.shape
    return pl.pallas_call(
        _win_kernel,
        grid=(depth, d // tr),
        in_specs=[pl.BlockSpec((None, tr, d_in), lambda l, i: (l, i, 0))],
        out_specs=pl.BlockSpec((None, tr, D_PROJ), lambda l, i: (l, i, 0)),
        out_shape=jax.ShapeDtypeStruct((depth, d, D_PROJ), BF16),
        compiler_params=_cparams(("parallel", "parallel")),
        name="w_in_layout",
    )(w)


def _prep_params(p):
    q = dict(p)
    for k in ("ffn1_w_down", "ffn2_w_down", "w_out"):
        q[k] = p[k].astype(BF16)
    for k in ("ffn1_w_up", "ffn2_w_up"):
        w = p[k].astype(BF16)
        q[k] = w.reshape(w.shape[0], w.shape[1], w.shape[2] // FFN_TF, FFN_TF).transpose(0, 2, 1, 3)
    q["w_in"] = _relayout_w_in(p["w_in"].astype(BF16))
    return q


def kernel(x_prompt, x_sample, state_ret, state_gdn, state_conv, norm_ffn1_w, ffn1_w_up, ffn1_w_down,
           norm_mix_w, w_in, ret_gn_w, gdn_conv_w, gdn_a_log, gdn_dt_bias, gdn_norm_w, gmlp_ws, gmlp_bs,
           gmlp_norm_w, w_out, norm_ffn2_w, ffn2_w_up, ffn2_w_down, final_norm_w):
    p = _prep_params({
        "norm_ffn1_w": norm_ffn1_w, "ffn1_w_up": ffn1_w_up, "ffn1_w_down": ffn1_w_down,
        "norm_mix_w": norm_mix_w, "w_in": w_in, "ret_gn_w": ret_gn_w, "gdn_conv_w": gdn_conv_w,
        "gdn_a_log": gdn_a_log, "gdn_dt_bias": gdn_dt_bias, "gdn_norm_w": gdn_norm_w,
        "gmlp_ws": gmlp_ws, "gmlp_bs": gmlp_bs, "gmlp_norm_w": gmlp_norm_w, "w_out": w_out,
        "norm_ffn2_w": norm_ffn2_w, "ffn2_w_up": ffn2_w_up, "ffn2_w_down": ffn2_w_down,
        "final_norm_w": final_norm_w})
    bp, tp = x_prompt.shape[0], x_prompt.shape[1]
    bs, ts = x_sample.shape[0], x_sample.shape[1]
    ret0 = jnp.zeros((DEPTH, bp, RET_HEADS, RET_DK, RET_DV), F32)
    gdn0 = jnp.zeros((DEPTH, bp, GDN_HEADS, GDN_DK, GDN_DV), F32)
    conv0 = jnp.zeros((DEPTH, bp, CONV_WIDTH - 1, GDN_CONV_CH), F32)
    pos_p = jnp.arange(tp, dtype=jnp.int32)
    y_p, ret_p, gdn_p, conv_p = _trunk(x_prompt, pos_p, ret0, gdn0, conv0, p, emit_vn=False)
    pos_s = PAST_LEN + jnp.arange(ts, dtype=jnp.int32)
    y_s, ret_s, gdn_s, conv_s, vn_s = _trunk(x_sample, pos_s, state_ret, state_gdn, state_conv, p, emit_vn=True)
    return (y_p, y_s, ret_p, ret_s, gdn_p, gdn_s, conv_p, conv_s, vn_s)
```

```python
import functools
import math

import jax
import jax.numpy as jnp
from jax import lax
from jax.experimental import pallas as pl
from jax.experimental.pallas import tpu as pltpu

F32 = jnp.float32
BF16 = jnp.bfloat16

DEPTH = 4
PAST_LEN = 4096
EPS = 1e-6
ROPE_BASE = 10000.0
RET_HEADS = 6
RET_DK = 64
RET_DV = 128
GDN_HEADS = 6
GDN_DK = 128
GDN_DV = 128
CONV_WIDTH = 4
GMLP_GROUPS = 4
GMLP_DIM = 128

RET_QK = RET_HEADS * RET_DK
RET_V = RET_HEADS * RET_DV
GDN_QK = GDN_HEADS * GDN_DK
GDN_V = GDN_HEADS * GDN_DV
GDN_CONV_CH = 2 * GDN_QK + GDN_V
GMLP_W = GMLP_GROUPS * GMLP_DIM
D_MIX = RET_V + GDN_V + GMLP_W

LANES = 128
SUBLANES = 8
CONV_HIST = CONV_WIDTH - 1
CONV_ROW0 = SUBLANES
TILE = 128
INV_BLOCK = 16
PROJ_COLS = 256
FILL_SERIES, FILL_MERGE, FILL_SOLVE, FILL_RECUR = (3, 0, 1), (2, 3, 1), (0, 1, 0), (1, 1, 0)

OFF_RQ = 0
OFF_RK = OFF_RQ + RET_QK
OFF_RV = OFF_RK + RET_QK
OFF_RG = OFF_RV + RET_V
OFF_GQKV = OFF_RG + RET_V
OFF_GZ = OFF_GQKV + GDN_CONV_CH
OFF_CU = OFF_GZ + GDN_V
OFF_CV = OFF_CU + GMLP_W
OFF_AB = OFF_CV + GMLP_W
D_PROJ = 6656

VMEM_LIMIT = 56 * 1024 * 1024


def _cparams(sem):
    return pltpu.CompilerParams(dimension_semantics=sem, vmem_limit_bytes=VMEM_LIMIT)


def _dot(a, b):
    return jnp.dot(a, b, preferred_element_type=F32)


def _dot_nt(a, b):
    return lax.dot_general(a, b, (((1,), (1,)), ((), ())), preferred_element_type=F32)


def _dot_tn(a, b):
    return lax.dot_general(a, b, (((0,), (0,)), ((), ())), preferred_element_type=F32)


def _silu(x):
    return x * jax.nn.sigmoid(x)


def _rms_rows(x, w):
    return x * lax.rsqrt(jnp.mean(x * x, axis=-1, keepdims=True) + EPS) * w


def _ffn_kernel(x_ref, nw_ref, wg_ref, wu_ref, wd_ref, *rest):
    fnw_ref = rest[0] if len(rest) == 3 else None
    o_ref, xn_ref = rest[-2:]
    j = pl.program_id(1)

    def delta():
        xn = xn_ref[...]
        g = _dot(xn, wg_ref[...])
        u = _dot(xn, wu_ref[...])
        a = (0.5 * _silu(g) * u).astype(BF16)
        return _dot(a, wd_ref[...])

    @pl.when(j == 0)
    def _():
        xn_ref[...] = _rms_rows(x_ref[...], nw_ref[...]).astype(BF16)
        o_ref[...] = x_ref[...] + delta()

    @pl.when(j != 0)
    def _():
        o_ref[...] += delta()

    if fnw_ref is not None:
        @pl.when(j == pl.num_programs(1) - 1)
        def _():
            o_ref[...] = _rms_rows(o_ref[...], fnw_ref[...])


def _ffn(x, nw, w_up, w_down, l, tm, tf, final_nw=None):
    n, d = x.shape
    dff = w_down.shape[1]
    nf = dff // tf
    vec = pl.BlockSpec((1, d), lambda i, j: (0, 0))
    return pl.pallas_call(
        _ffn_kernel,
        grid=(n // tm, nf),
        in_specs=[
            pl.BlockSpec((tm, d), lambda i, j: (i, 0)),
            vec,
            pl.BlockSpec((None, d, tf), lambda i, j: (l, 0, j)),
            pl.BlockSpec((None, d, tf), lambda i, j: (l, 0, j + nf)),
            pl.BlockSpec((None, tf, d), lambda i, j: (l, j, 0)),
        ] + ([] if final_nw is None else [vec]),
        out_specs=pl.BlockSpec((tm, d), lambda i, j: (i, 0)),
        out_shape=jax.ShapeDtypeStruct((n, d), F32),
        scratch_shapes=[pltpu.VMEM((tm, d), BF16)],
        compiler_params=_cparams(("parallel", "arbitrary")),
        name="ffn",
    )(x, nw, w_up, w_up, w_down, *(() if final_nw is None else (final_nw,)))


def _outproj_kernel(x_ref, c_ref, w_ref, o_ref):
    o_ref[...] = x_ref[...] + _dot(c_ref[...], w_ref[...])


def _outproj(x, cat, w, l, tm):
    n, d = x.shape
    k = cat.shape[1]
    return pl.pallas_call(
        _outproj_kernel,
        grid=(n // tm,),
        in_specs=[
            pl.BlockSpec((tm, d), lambda i: (i, 0)),
            pl.BlockSpec((tm, k), lambda i: (i, 0)),
            pl.BlockSpec((None, k, d), lambda i: (l, 0, 0)),
        ],
        out_specs=pl.BlockSpec((tm, d), lambda i: (i, 0)),
        out_shape=jax.ShapeDtypeStruct((n, d), F32),
        compiler_params=_cparams(("parallel",)),
        name="outproj",
    )(x, cat, w)


def _split3(x):
    hi = x.astype(BF16)
    r1 = x - hi.astype(F32)
    mid = r1.astype(BF16)
    lo = (r1 - mid.astype(F32)).astype(BF16)
    return hi, mid, lo


def _mixer_kernel(valid, emit_vn, pipelined,
                  x0_ref, x1_ref, nw_ref, win_ref,
                  cos_ref, sin_ref, dmat_ref, qdec_ref, kdec_ref, gam_ref,
                  alog_ref, dtb_ref, convw_ref, rgn_ref, gnw_ref, mnw_ref, ws_ref, bs_ref,
                  rs0_ref, gs0_ref, cv0_ref,
                  cat_ref, rs_ref, gs_ref, cvn_ref, *rest):
    if emit_vn:
        vn_ref = rest[0]
        rest = rest[1:]
    else:
        vn_ref = None
    proj_s, xn_s, sret, sgdn, xbuf, gq_s, gk_s, gv_s, l_s, gcum_s, beta_s, uv_s, a1_s, a2_s = rest
    tt = TILE
    t = pl.program_id(1)
    nt = pl.num_programs(1)
    cur = t % 2
    heads = range(GDN_HEADS)
    col_blocks = [slice(PROJ_COLS * c, PROJ_COLS * (c + 1)) for c in range(D_PROJ // PROJ_COLS)]

    lane = lax.broadcasted_iota(jnp.int32, (tt, LANES), 1)
    ri = lax.broadcasted_iota(jnp.int32, (tt, tt), 0)
    ci = lax.broadcasted_iota(jnp.int32, (tt, tt), 1)
    tri = ri >= ci
    strict = ri > ci
    eye = jnp.where(ri == ci, 1.0, 0.0).astype(F32)

    def conv_load(slot, carry):
        h0 = CONV_ROW0 - CONV_HIST
        if carry:
            xbuf[h0:CONV_ROW0, :] = xbuf[h0 + tt:CONV_ROW0 + tt, :]
        xbuf[CONV_ROW0:CONV_ROW0 + tt, :] = proj_s[slot, :, OFF_GQKV:OFF_GQKV + GDN_CONV_CH]
        cvn_ref[0] = xbuf[h0 + valid:CONV_ROW0 + valid, :]

    def conv_piece(slot, p):
        cs = slice(LANES * p, LANES * (p + 1))
        xc = xbuf[:, cs]
        y = xc * convw_ref[0:1, cs]
        for i in range(1, CONV_WIDTH):
            y = pltpu.roll(y, 1, 0) + xc * convw_ref[i:i + 1, cs]
        y = _silu(y[CONV_ROW0:CONV_ROW0 + tt])
        kind, h = divmod(p, GDN_HEADS)
        if kind == 2:
            gv_s[slot, h] = y
        else:
            yn = y * lax.rsqrt(jnp.sum(y * y, axis=-1, keepdims=True) + EPS)
            if kind == 0:
                gq_s[slot, h] = yn * (GDN_DK ** -0.5)
            else:
                gk_s[slot, h] = yn

    def gates(slot, env):
        ab = proj_s[slot, :, OFF_AB:OFF_AB + LANES]
        log_a = -jnp.exp(alog_ref[...]) * jax.nn.softplus(ab + dtb_ref[...])
        beta = jax.nn.sigmoid(ab)
        if valid < tt:
            real = lax.broadcasted_iota(jnp.int32, (tt, LANES), 0) < valid
            log_a = jnp.where(real, log_a, 0.0)
            beta = jnp.where(real, beta, 0.0)
        cum_mat = jnp.where(tri, 1.0, 0.0).astype(BF16)
        la_hi, la_mid, la_lo = _split3(log_a)
        gcum = (_dot(cum_mat, la_hi) + _dot(cum_mat, la_mid)) + _dot(cum_mat, la_lo)
        gcum_s[slot] = gcum
        beta_s[slot] = beta
        env["gcum"], env["gcum_t"], env["beta"] = gcum, gcum.T, beta

    def system_matrix(slot, env, h):
        gcol = env["gcum"][:, h:h + 1]
        grow = env["gcum_t"][h:h + 1, :]
        decay = jnp.exp(jnp.where(tri, gcol - grow, -1e30))
        bcol = env["beta"][:, GDN_HEADS + h:GDN_HEADS + h + 1]
        kb = gk_s[slot, h].astype(BF16)
        r = _dot_nt(jnp.concatenate([gq_s[slot, h].astype(BF16), kb], axis=0), kb)
        a2_s[slot, h, 0:tt, :] = (r[:tt] * decay).astype(BF16)
        l_s[slot, h] = bcol * r[tt:] * jnp.where(strict, decay, 0.0)

    def prework(slot, carry):
        env = {}
        items = [functools.partial(conv_load, slot, carry)]
        items += [functools.partial(conv_piece, slot, p) for p in range(GDN_CONV_CH // LANES)]
        items += [functools.partial(gates, slot, env)]
        items += [functools.partial(system_matrix, slot, env, h) for h in heads]
        return items

    @pl.when(t == 0)
    def _():
        for h in range(RET_HEADS):
            r = (h % 2) * RET_DK
            sret[h] = jnp.zeros((2 * RET_DK, RET_DV), F32)
            sret[h, r:r + RET_DK, :] = rs0_ref[0, h]
        for h in heads:
            sgdn[h] = gs0_ref[0, h]
        h0 = CONV_ROW0 - CONV_HIST
        xbuf[0:h0, :] = jnp.zeros((h0, GDN_CONV_CH), F32)
        xbuf[h0:CONV_ROW0, :] = cv0_ref[0]
        xn0 = _rms_rows(x0_ref[0], nw_ref[...]).astype(BF16)
        for cs in col_blocks:
            proj_s[0, :, cs] = _dot(xn0, win_ref[:, cs])
        for item in prework(0, False):
            item()

    def proj(off, width):
        return proj_s[cur, :, off:off + width]

    def project_next(cs):
        proj_s[1 - cur, :, cs] = _dot(xn_s[...], win_ref[:, cs])

    first_half = (lane % RET_DK) < (RET_DK // 2)

    def rope(x):
        rot = jnp.where(first_half, pltpu.roll(x, LANES - RET_DK // 2, 1), pltpu.roll(x, RET_DK // 2, 1))
        return x * cos_ref[...] + rot * sin_ref[...]

    def retention_pair(j):
        qp = rope(proj(OFF_RQ + LANES * j, LANES))
        kp = rope(proj(OFF_RK + LANES * j, LANES)) * (RET_DK ** -0.5)
        qpb = qp.astype(BF16)
        for e in range(2):
            h = 2 * j + e
            km = jnp.where(lane >= RET_DK if e else lane < RET_DK, kp, 0.0)
            v = proj(OFF_RV + RET_DV * h, RET_DV).astype(BF16)
            s_prev = sret[h]
            scores = _dot_nt(qpb, km.astype(BF16)) * dmat_ref[h]
            lhs = jnp.concatenate([scores.astype(BF16), (qp * qdec_ref[h]).astype(BF16)], axis=1)
            o = _dot(lhs, jnp.concatenate([v, s_prev.astype(BF16)], axis=0))
            sret[h] = gam_ref[h] * s_prev + _dot_tn((km * kdec_ref[h]).astype(BF16), v)
            mu = jnp.mean(o, axis=-1, keepdims=True)
            oc = o - mu
            var = jnp.mean(oc * oc, axis=-1, keepdims=True)
            y = oc * lax.rsqrt(var + EPS) * rgn_ref[:, RET_DV * h:RET_DV * (h + 1)]
            cat_ref[0, :, RET_DV * h:RET_DV * (h + 1)] = (y * _silu(proj(OFF_RG + RET_DV * h, RET_DV))).astype(BF16)

    def gmlp_group(g):
        cs = slice(GMLP_DIM * g, GMLP_DIM * (g + 1))
        v = proj(OFF_CV + GMLP_DIM * g, GMLP_DIM)
        mu = jnp.mean(v, axis=-1, keepdims=True)
        vc = v - mu
        var = jnp.mean(vc * vc, axis=-1, keepdims=True)
        vg = vc * lax.rsqrt(var + EPS) * mnw_ref[:, cs]
        if emit_vn:
            vn_ref[0, :, cs] = vg
        wtri = jnp.where(tri, ws_ref[g], 0.0).astype(BF16)
        z = _dot(wtri, vg.astype(BF16)) + bs_ref[g]
        u = proj(OFF_CU + GMLP_DIM * g, GMLP_DIM)
        cat_ref[0, :, RET_V + GDN_V + GMLP_DIM * g:RET_V + GDN_V + GMLP_DIM * (g + 1)] = (u * z).astype(BF16)

    side = [functools.partial(retention_pair, j) for j in range(RET_HEADS // 2)]
    side += [functools.partial(gmlp_group, g) for g in range(GMLP_GROUPS)]
    dense, ahead = [], []
    if pipelined:
        first = [c for c in range(len(col_blocks))
                 if OFF_GQKV <= PROJ_COLS * c < OFF_GQKV + GDN_CONV_CH or PROJ_COLS * c <= OFF_AB < PROJ_COLS * (c + 1)]
        order = first + [c for c in range(len(col_blocks)) if c not in first]
        dense = [functools.partial(project_next, col_blocks[c]) for c in order]
        n_first = len(first)
        ahead = prework(1 - cur, True)

    issued = [0]

    def fill(n_dense, n_ahead=0, n_side=0):
        for _ in range(n_dense):
            if dense:
                dense.pop(0)()
                issued[0] += 1
        for _ in range(n_ahead):
            if ahead and issued[0] >= n_first:
                ahead.pop(0)()
        for _ in range(n_side):
            if side:
                side.pop(0)()

    if pipelined:
        xn_s[...] = _rms_rows(x1_ref[0], nw_ref[...]).astype(BF16)

    zero_b = jnp.zeros((tt, tt), BF16)

    def pair_dot(a2, b2):
        bb = b2.astype(BF16)
        rhs = jnp.concatenate([jnp.concatenate([bb[:, :tt], zero_b], axis=1),
                               jnp.concatenate([zero_b, bb[:, tt:]], axis=1)], axis=0)
        return _dot(a2.astype(BF16), rhs)

    def same_block(b):
        return (ri // b) == (ci // b)

    pairs = range(GDN_HEADS // 2)
    eye2 = jnp.concatenate([eye, eye], axis=1)
    l2 = [jnp.concatenate([l_s[cur, 2 * j], l_s[cur, 2 * j + 1]], axis=1) for j in pairs]
    diag = same_block(INV_BLOCK)
    diag2 = jnp.concatenate([diag, diag], axis=1)
    pw = [jnp.where(diag2, -l2[j], 0.0) for j in pairs]
    td = [eye2 + pw[j] for j in pairs]
    n_fac = max(int(math.ceil(math.log2(min(INV_BLOCK, valid)))), 1)
    for it in range(n_fac):
        if it == 0:
            if n_fac > 1:
                pw = [pair_dot(pw[j], pw[j]) for j in pairs]
        elif it == n_fac - 1:
            td = [td[j] + pair_dot(td[j], pw[j]) for j in pairs]
        else:
            r = [pair_dot(jnp.concatenate([pw[j], td[j]], axis=0), pw[j]) for j in pairs]
            pw = [r[j][:tt] for j in pairs]
            td = [td[j] + r[j][tt:] for j in pairs]
        fill(*FILL_SERIES)
    blk = INV_BLOCK
    while blk < valid:
        off = same_block(2 * blk) & jnp.logical_not(same_block(blk))
        off2 = jnp.concatenate([off, off], axis=1)
        y = [pair_dot(jnp.where(off2, l2[j], 0.0), td[j]) for j in pairs]
        fill(*FILL_MERGE)
        z = [pair_dot(td[j], y[j]) for j in pairs]
        td = [td[j] - z[j] for j in pairs]
        fill(*FILL_MERGE)
        blk *= 2

    gcum = gcum_s[cur]
    beta = beta_s[cur]
    gls = []
    for h in heads:
        j, e = divmod(h, 2)
        tinv = td[j][:, tt * e:tt * (e + 1)]
        gcol = gcum[:, h:h + 1]
        eg = jnp.exp(gcol)
        bcol = beta[:, GDN_HEADS + h:GDN_HEADS + h + 1]
        k = gk_s[cur, h]
        rhs = jnp.concatenate([bcol * gv_s[cur, h], (bcol * eg) * k], axis=1).astype(BF16)
        sol = _dot(tinv.astype(BF16), rhs)
        uv_s[h] = sol[:, :GDN_DV]
        a1_s[h, 0:tt, :] = sol[:, GDN_DV:].astype(BF16)
        a1_s[h, tt:2 * tt, :] = (gq_s[cur, h] * eg).astype(BF16)
        glast = gcum[tt - 1:tt, h:h + 1]
        a2_s[cur, h, tt:2 * tt, :] = (k * jnp.exp(glast - gcol)).T.astype(BF16)
        gls.append(jnp.exp(glast))
        fill(*FILL_SOLVE)

    ss = [sgdn[h] for h in heads]
    r1 = [_dot(a1_s[h], ss[h].astype(BF16)) for h in heads]
    fill(*FILL_RECUR)
    ub = [(uv_s[h] - r1[h][:tt]).astype(BF16) for h in heads]
    r2 = [_dot(a2_s[cur, h], ub[h]) for h in heads]
    fill(*FILL_RECUR)
    for h in heads:
        sgdn[h] = ss[h] * gls[h] + r2[h][tt:]
        o = r1[h][tt:] + r2[h][:tt]
        y = o * lax.rsqrt(jnp.mean(o * o, axis=-1, keepdims=True) + EPS) * gnw_ref[...]
        cat_ref[0, :, RET_V + GDN_DV * h:RET_V + GDN_DV * (h + 1)] = (y * _silu(proj(OFF_GZ + GDN_DV * h, GDN_DV))).astype(BF16)

    fill(len(dense), 0, 0)
    fill(0, len(ahead), len(side))

    @pl.when(t == nt - 1)
    def _():
        for h in range(RET_HEADS):
            r = (h % 2) * RET_DK
            rs_ref[0, h] = sret[h, r:r + RET_DK, :]
        for h in heads:
            gs_ref[0, h] = sgdn[h]


def _mixer(x, nw, w_in, l, cosf, sinf, consts, lp, rs0, gs0, cv0, valid, emit_vn):
    b, t, d = x.shape
    tt = TILE
    nt = t // tt
    dmat, qdec, kdec, gam = consts
    full = lambda a: pl.BlockSpec(a.shape, lambda i, j, _n=a.ndim: (0,) * _n)
    small = [cosf, sinf, dmat, qdec, kdec, gam,
             lp["alog"], lp["dtb"], lp["convw"], lp["rgn"], lp["gnw"], lp["mnw"], lp["ws"], lp["bs"]]
    state_specs = [
        pl.BlockSpec((1, RET_HEADS, RET_DK, RET_DV), lambda i, j: (i, 0, 0, 0)),
        pl.BlockSpec((1, GDN_HEADS, GDN_DK, GDN_DV), lambda i, j: (i, 0, 0, 0)),
        pl.BlockSpec((1, CONV_WIDTH - 1, GDN_CONV_CH), lambda i, j: (i, 0, 0)),
    ]
    in_specs = [
        pl.BlockSpec((1, tt, d), lambda i, j: (i, 0, 0)),
        pl.BlockSpec((1, tt, d), lambda i, j: (i, jnp.minimum(j + 1, nt - 1), 0)),
        pl.BlockSpec((1, d), lambda i, j: (0, 0)),
        pl.BlockSpec((None, d, D_PROJ), lambda i, j: (l, 0, 0), pipeline_mode=pl.Buffered(1)),
        pl.BlockSpec((tt, LANES), lambda i, j: (j, 0)),
        pl.BlockSpec((tt, LANES), lambda i, j: (j, 0)),
    ] + [full(a) for a in small[2:]] + state_specs
    out_shape = [
        jax.ShapeDtypeStruct((b, t, D_MIX), BF16),
        jax.ShapeDtypeStruct((b, RET_HEADS, RET_DK, RET_DV), F32),
        jax.ShapeDtypeStruct((b, GDN_HEADS, GDN_DK, GDN_DV), F32),
        jax.ShapeDtypeStruct((b, CONV_WIDTH - 1, GDN_CONV_CH), F32),
    ]
    out_specs = [pl.BlockSpec((1, tt, D_MIX), lambda i, j: (i, j, 0))] + state_specs
    if emit_vn:
        out_shape.append(jax.ShapeDtypeStruct((b, t, GMLP_W), F32))
        out_specs.append(pl.BlockSpec((1, tt, GMLP_W), lambda i, j: (i, j, 0)))
    hs = (GDN_HEADS, tt, LANES)
    scratch = [
        pltpu.VMEM((2, tt, D_PROJ), F32),
        pltpu.VMEM((tt, d), BF16),
        pltpu.VMEM((RET_HEADS, 2 * RET_DK, RET_DV), F32),
        pltpu.VMEM((GDN_HEADS, GDN_DK, GDN_DV), F32),
        pltpu.VMEM((CONV_ROW0 + tt, GDN_CONV_CH), F32),
        pltpu.VMEM((2,) + hs, F32), pltpu.VMEM((2,) + hs, F32), pltpu.VMEM((2,) + hs, F32),
        pltpu.VMEM((2, GDN_HEADS, tt, tt), F32),
        pltpu.VMEM((2, tt, LANES), F32), pltpu.VMEM((2, tt, LANES), F32),
        pltpu.VMEM(hs, F32),
        pltpu.VMEM((GDN_HEADS, 2 * tt, LANES), BF16),
        pltpu.VMEM((2, GDN_HEADS, 2 * tt, LANES), BF16),
    ]
    return pl.pallas_call(
        functools.partial(_mixer_kernel, valid, emit_vn, nt > 1),
        grid=(b, nt),
        in_specs=in_specs,
        out_specs=out_specs,
        out_shape=out_shape,
        scratch_shapes=scratch,
        compiler_params=_cparams(("parallel", "arbitrary")),
        name="mixer",
    )(x, x, nw, w_in, *small, rs0, gs0, cv0)


def _pad_rows(a, rows, axis):
    pad = [(0, 0)] * a.ndim
    pad[axis] = (0, rows - a.shape[axis])
    return jnp.pad(a, pad)


def _rope_tables(pos):
    half = RET_DK // 2
    inv = jnp.power(ROPE_BASE, -jnp.arange(half, dtype=F32) / half)
    ang = pos.astype(F32)[:, None] * inv[None, :]
    cos, sin = jnp.cos(ang), jnp.sin(ang)
    reps = LANES // RET_DK
    return jnp.tile(jnp.concatenate([cos, cos], axis=1), (1, reps)), jnp.tile(jnp.concatenate([-sin, sin], axis=1), (1, reps))


def _ret_consts(cr):
    log_gamma = jnp.log1p(-jnp.power(2.0, -5.0 - jnp.arange(RET_HEADS, dtype=F32)))
    idx = jnp.arange(cr, dtype=F32)
    diff = idx[:, None] - idx[None, :]
    causal = diff >= 0
    dmat = jnp.where(causal[None], jnp.exp(jnp.where(causal, diff, 0.0)[None] * log_gamma[:, None, None]), 0.0)
    kdec = jnp.exp(log_gamma[:, None] * (cr - 1 - idx)[None, :])[:, :, None]
    qdec = jnp.exp(log_gamma[:, None] * (idx + 1.0)[None, :])[:, :, None]
    gam = jnp.broadcast_to(jnp.exp(cr * log_gamma)[:, None, None], (RET_HEADS, 1, LANES))
    dmat = _pad_rows(_pad_rows(dmat, TILE, 1), TILE, 2)
    return dmat, _pad_rows(qdec, TILE, 1), _pad_rows(kdec, TILE, 1), gam


def _pad_lanes(v):
    return jnp.pad(v, (0, LANES - v.shape[0]))[None, :]


def _tiles(n):
    return dict(tm_ffn=min(1024, n), tf=512, tm_out=min(512, n))


def _trunk(x, pos, rs0, gs0, cv0, p, emit_vn):
    b, t, d = x.shape
    n = b * t
    tl = _tiles(n)
    valid = min(t, TILE)
    tpad = -(-t // TILE) * TILE
    cosf, sinf = _rope_tables(pos)
    cosf, sinf = _pad_rows(cosf, tpad, 0), _pad_rows(sinf, tpad, 0)
    consts = _ret_consts(valid)
    x = x.reshape(n, d)
    outs = []
    for l in range(DEPTH):
        x = _ffn(x, p["norm_ffn1_w"][l][None], p["ffn1_w_up"], p["ffn1_w_down"], l, tl["tm_ffn"], tl["tf"])
        lp = {
            "alog": _pad_lanes(p["gdn_a_log"][l]), "dtb": _pad_lanes(p["gdn_dt_bias"][l]),
            "convw": p["gdn_conv_w"][l], "rgn": p["ret_gn_w"][l][None], "gnw": p["gdn_norm_w"][l][None],
            "mnw": p["gmlp_norm_w"][l][None],
            "ws": _pad_rows(_pad_rows(p["gmlp_ws"][l][:, :valid, :valid], TILE, 1), TILE, 2),
            "bs": _pad_rows(p["gmlp_bs"][l][:, :valid, None], TILE, 1),
        }
        res = _mixer(_pad_rows(x.reshape(b, t, d), tpad, 1), p["norm_mix_w"][l][None], p["w_in"], l,
                     cosf, sinf, consts, lp, rs0[l], gs0[l], cv0[l], valid, emit_vn)
        cat = res[0][:, :t].reshape(n, D_MIX)
        x = _outproj(x, cat, p["w_out"], l, tl["tm_out"])
        x = _ffn(x, p["norm_ffn2_w"][l][None], p["ffn2_w_up"], p["ffn2_w_down"], l, tl["tm_ffn"], tl["tf"],
                 final_nw=p["final_norm_w"][None] if l == DEPTH - 1 else None)
        outs.append(tuple(res[1:4]) + ((res[4][:, :t],) if emit_vn else ()))
    y = x.reshape(b, t, d)
    return (y,) + tuple(jnp.stack([o[i] for o in outs]) for i in range(len(outs[0])))


def _win_kernel(w_ref, o_ref):
    ab0 = OFF_CU
    nab = 2 * GDN_HEADS
    o_ref[:, :ab0] = w_ref[:, :ab0].astype(BF16)
    o_ref[:, OFF_CU:OFF_AB] = w_ref[:, ab0 + nab:].astype(BF16)
    o_ref[:, OFF_AB:] = jnp.zeros((o_ref.shape[0], D_PROJ - OFF_AB), BF16)
    o_ref[:, OFF_AB:OFF_AB + nab] = w_ref[:, ab0:ab0 + nab].astype(BF16)


def _relayout_w_in(w, tr=256):
    depth, d, d_in = w.shape
    return pl.pallas_call(
        _win_kernel,
        grid=(depth, d // tr),
        in_specs=[pl.BlockSpec((None, tr, d_in), lambda l, i: (l, i, 0))],
        out_specs=pl.BlockSpec((None, tr, D_PROJ), lambda l, i: (l, i, 0)),
        out_shape=jax.ShapeDtypeStruct((depth, d, D_PROJ), BF16),
        compiler_params=_cparams(("parallel", "parallel")),
        name="w_in_layout",
    )(w)


def _prep_params(p):
    q = dict(p)
    for k in ("ffn1_w_up", "ffn1_w_down", "ffn2_w_up", "ffn2_w_down", "w_out"):
        q[k] = p[k].astype(BF16)
    q["w_in"] = _relayout_w_in(p["w_in"])
    return q


def kernel(x_prompt, x_sample, state_ret, state_gdn, state_conv, norm_ffn1_w, ffn1_w_up, ffn1_w_down,
           norm_mix_w, w_in, ret_gn_w, gdn_conv_w, gdn_a_log, gdn_dt_bias, gdn_norm_w, gmlp_ws, gmlp_bs,
           gmlp_norm_w, w_out, norm_ffn2_w, ffn2_w_up, ffn2_w_down, final_norm_w):
    p = _prep_params({
        "norm_ffn1_w": norm_ffn1_w, "ffn1_w_up": ffn1_w_up, "ffn1_w_down": ffn1_w_down,
        "norm_mix_w": norm_mix_w, "w_in": w_in, "ret_gn_w": ret_gn_w, "gdn_conv_w": gdn_conv_w,
        "gdn_a_log": gdn_a_log, "gdn_dt_bias": gdn_dt_bias, "gdn_norm_w": gdn_norm_w,
        "gmlp_ws": gmlp_ws, "gmlp_bs": gmlp_bs, "gmlp_norm_w": gmlp_norm_w, "w_out": w_out,
        "norm_ffn2_w": norm_ffn2_w, "ffn2_w_up": ffn2_w_up, "ffn2_w_down": ffn2_w_down,
        "final_norm_w": final_norm_w})
    bp, tp = x_prompt.shape[0], x_prompt.shape[1]
    bs, ts = x_sample.shape[0], x_sample.shape[1]
    ret0 = jnp.zeros((DEPTH, bp, RET_HEADS, RET_DK, RET_DV), F32)
    gdn0 = jnp.zeros((DEPTH, bp, GDN_HEADS, GDN_DK, GDN_DV), F32)
    conv0 = jnp.zeros((DEPTH, bp, CONV_WIDTH - 1, GDN_CONV_CH), F32)
    pos_p = jnp.arange(tp, dtype=jnp.int32)
    y_p, ret_p, gdn_p, conv_p = _trunk(x_prompt, pos_p, ret0, gdn0, conv0, p, emit_vn=False)
    pos_s = PAST_LEN + jnp.arange(ts, dtype=jnp.int32)
    y_s, ret_s, gdn_s, conv_s, vn_s = _trunk(x_sample, pos_s, state_ret, state_gdn, state_conv, p, emit_vn=True)
    return (y_p, y_s, ret_p, ret_s, gdn_p, gdn_s, conv_p, conv_s, vn_s)
```

```python
import functools
import math

import jax
import jax.numpy as jnp
from jax import lax
from jax.experimental import pallas as pl
from jax.experimental.pallas import tpu as pltpu

F32 = jnp.float32
BF16 = jnp.bfloat16

DEPTH = 4
PAST_LEN = 4096
EPS = 1e-6
ROPE_BASE = 10000.0
RET_HEADS = 6
RET_DK = 64
RET_DV = 128
GDN_HEADS = 6
GDN_DK = 128
GDN_DV = 128
CONV_WIDTH = 4
GMLP_GROUPS = 4
GMLP_DIM = 128

RET_QK = RET_HEADS * RET_DK
RET_V = RET_HEADS * RET_DV
GDN_QK = GDN_HEADS * GDN_DK
GDN_V = GDN_HEADS * GDN_DV
GDN_CONV_CH = 2 * GDN_QK + GDN_V
GMLP_W = GMLP_GROUPS * GMLP_DIM
D_MIX = RET_V + GDN_V + GMLP_W

LANES = 128
SUBLANES = 8
CONV_HIST = CONV_WIDTH - 1
CONV_ROW0 = SUBLANES
TILE = 128
INV_BLOCK = 16
PROJ_COLS = 256
FILL_SERIES, FILL_MERGE, FILL_SOLVE, FILL_RECUR = (3, 0, 1), (2, 3, 1), (0, 1, 0), (1, 1, 0)

OFF_RQ = 0
OFF_RK = OFF_RQ + RET_QK
OFF_RV = OFF_RK + RET_QK
OFF_RG = OFF_RV + RET_V
OFF_GQKV = OFF_RG + RET_V
OFF_GZ = OFF_GQKV + GDN_CONV_CH
OFF_CU = OFF_GZ + GDN_V
OFF_CV = OFF_CU + GMLP_W
OFF_AB = OFF_CV + GMLP_W
D_PROJ = 6656

VMEM_LIMIT = 56 * 1024 * 1024


def _cparams(sem):
    return pltpu.CompilerParams(dimension_semantics=sem, vmem_limit_bytes=VMEM_LIMIT)


def _dot(a, b):
    return jnp.dot(a, b, preferred_element_type=F32)


def _dot_nt(a, b):
    return lax.dot_general(a, b, (((1,), (1,)), ((), ())), preferred_element_type=F32)


def _dot_tn(a, b):
    return lax.dot_general(a, b, (((0,), (0,)), ((), ())), preferred_element_type=F32)


def _silu(x):
    return x * jax.nn.sigmoid(x)


def _rms_rows(x, w):
    return x * lax.rsqrt(jnp.mean(x * x, axis=-1, keepdims=True) + EPS) * w


def _ffn_kernel(x_ref, nw_ref, wg_ref, wu_ref, wd_ref, *rest):
    fnw_ref = rest[0] if len(rest) == 3 else None
    o_ref, xn_ref = rest[-2:]
    j = pl.program_id(1)

    def delta():
        xn = xn_ref[...]
        g = _dot(xn, wg_ref[...])
        u = _dot(xn, wu_ref[...])
        a = (0.5 * _silu(g) * u).astype(BF16)
        return _dot(a, wd_ref[...])

    @pl.when(j == 0)
    def _():
        xn_ref[...] = _rms_rows(x_ref[...], nw_ref[...]).astype(BF16)
        o_ref[...] = x_ref[...] + delta()

    @pl.when(j != 0)
    def _():
        o_ref[...] += delta()

    if fnw_ref is not None:
        @pl.when(j == pl.num_programs(1) - 1)
        def _():
            o_ref[...] = _rms_rows(o_ref[...], fnw_ref[...])


def _ffn(x, nw, w_up, w_down, l, tm, tf, final_nw=None):
    n, d = x.shape
    dff = w_down.shape[1]
    nf = dff // tf
    vec = pl.BlockSpec((1, d), lambda i, j: (0, 0))
    return pl.pallas_call(
        _ffn_kernel,
        grid=(n // tm, nf),
        in_specs=[
            pl.BlockSpec((tm, d), lambda i, j: (i, 0)),
            vec,
            pl.BlockSpec((None, d, tf), lambda i, j: (l, 0, j)),
            pl.BlockSpec((None, d, tf), lambda i, j: (l, 0, j + nf)),
            pl.BlockSpec((None, tf, d), lambda i, j: (l, j, 0)),
        ] + ([] if final_nw is None else [vec]),
        out_specs=pl.BlockSpec((tm, d), lambda i, j: (i, 0)),
        out_shape=jax.ShapeDtypeStruct((n, d), F32),
        scratch_shapes=[pltpu.VMEM((tm, d), BF16)],
        compiler_params=_cparams(("parallel", "arbitrary")),
        name="ffn",
    )(x, nw, w_up, w_up, w_down, *(() if final_nw is None else (final_nw,)))


def _outproj_kernel(x_ref, c_ref, w_ref, o_ref):
    o_ref[...] = x_ref[...] + _dot(c_ref[...], w_ref[...])


def _outproj(x, cat, w, l, tm):
    n, d = x.shape
    k = cat.shape[1]
    return pl.pallas_call(
        _outproj_kernel,
        grid=(n // tm,),
        in_specs=[
            pl.BlockSpec((tm, d), lambda i: (i, 0)),
            pl.BlockSpec((tm, k), lambda i: (i, 0)),
            pl.BlockSpec((None, k, d), lambda i: (l, 0, 0)),
        ],
        out_specs=pl.BlockSpec((tm, d), lambda i: (i, 0)),
        out_shape=jax.ShapeDtypeStruct((n, d), F32),
        compiler_params=_cparams(("parallel",)),
        name="outproj",
    )(x, cat, w)


def _split3(x):
    hi = x.astype(BF16)
    r1 = x - hi.astype(F32)
    mid = r1.astype(BF16)
    lo = (r1 - mid.astype(F32)).astype(BF16)
    return hi, mid, lo


def _mixer_kernel(valid, emit_vn, pipelined,
                  x0_ref, x1_ref, nw_ref, win_ref,
                  cos_ref, sin_ref, dmat_ref, qdec_ref, kdec_ref, gam_ref,
                  alog_ref, dtb_ref, convw_ref, rgn_ref, gnw_ref, mnw_ref, ws_ref, bs_ref,
                  rs0_ref, gs0_ref, cv0_ref,
                  cat_ref, rs_ref, gs_ref, cvn_ref, *rest):
    if emit_vn:
        vn_ref = rest[0]
        rest = rest[1:]
    else:
        vn_ref = None
    proj_s, xn_s, sret, sgdn, xbuf, gq_s, gk_s, gv_s, l_s, gcum_s, beta_s, uv_s, a1_s, a2_s = rest
    tt = TILE
    t = pl.program_id(1)
    nt = pl.num_programs(1)
    cur = t % 2
    heads = range(GDN_HEADS)
    col_blocks = [slice(PROJ_COLS * c, PROJ_COLS * (c + 1)) for c in range(D_PROJ // PROJ_COLS)]

    lane = lax.broadcasted_iota(jnp.int32, (tt, LANES), 1)
    ri = lax.broadcasted_iota(jnp.int32, (tt, tt), 0)
    ci = lax.broadcasted_iota(jnp.int32, (tt, tt), 1)
    tri = ri >= ci
    strict = ri > ci
    eye = jnp.where(ri == ci, 1.0, 0.0).astype(F32)

    def conv_load(slot, carry):
        h0 = CONV_ROW0 - CONV_HIST
        if carry:
            xbuf[h0:CONV_ROW0, :] = xbuf[h0 + tt:CONV_ROW0 + tt, :]
        xbuf[CONV_ROW0:CONV_ROW0 + tt, :] = proj_s[slot, :, OFF_GQKV:OFF_GQKV + GDN_CONV_CH]
        cvn_ref[0] = xbuf[h0 + valid:CONV_ROW0 + valid, :]

    def conv_piece(slot, p):
        cs = slice(LANES * p, LANES * (p + 1))
        xc = xbuf[:, cs]
        y = xc * convw_ref[0:1, cs]
        for i in range(1, CONV_WIDTH):
            y = pltpu.roll(y, 1, 0) + xc * convw_ref[i:i + 1, cs]
        y = _silu(y[CONV_ROW0:CONV_ROW0 + tt])
        kind, h = divmod(p, GDN_HEADS)
        if kind == 2:
            gv_s[slot, h] = y
        else:
            yn = y * lax.rsqrt(jnp.sum(y * y, axis=-1, keepdims=True) + EPS)
            if kind == 0:
                gq_s[slot, h] = yn * (GDN_DK ** -0.5)
            else:
                gk_s[slot, h] = yn

    def gates(slot, env):
        ab = proj_s[slot, :, OFF_AB:OFF_AB + LANES]
        log_a = -jnp.exp(alog_ref[...]) * jax.nn.softplus(ab + dtb_ref[...])
        beta = jax.nn.sigmoid(ab)
        if valid < tt:
            real = lax.broadcasted_iota(jnp.int32, (tt, LANES), 0) < valid
            log_a = jnp.where(real, log_a, 0.0)
            beta = jnp.where(real, beta, 0.0)
        cum_mat = jnp.where(tri, 1.0, 0.0).astype(BF16)
        la_hi, la_mid, la_lo = _split3(log_a)
        gcum = (_dot(cum_mat, la_hi) + _dot(cum_mat, la_mid)) + _dot(cum_mat, la_lo)
        gcum_s[slot] = gcum
        beta_s[slot] = beta
        env["gcum"], env["gcum_t"], env["beta"] = gcum, gcum.T, beta

    def system_matrix(slot, env, h):
        gcol = env["gcum"][:, h:h + 1]
        grow = env["gcum_t"][h:h + 1, :]
        decay = jnp.exp(jnp.where(tri, gcol - grow, -1e30))
        bcol = env["beta"][:, GDN_HEADS + h:GDN_HEADS + h + 1]
        kb = gk_s[slot, h].astype(BF16)
        r = _dot_nt(jnp.concatenate([gq_s[slot, h].astype(BF16), kb], axis=0), kb)
        a2_s[slot, h, 0:tt, :] = (r[:tt] * decay).astype(BF16)
        l_s[slot, h] = bcol * r[tt:] * jnp.where(strict, decay, 0.0)

    def prework(slot, carry):
        env = {}
        items = [functools.partial(conv_load, slot, carry)]
        items += [functools.partial(conv_piece, slot, p) for p in range(GDN_CONV_CH // LANES)]
        items += [functools.partial(gates, slot, env)]
        items += [functools.partial(system_matrix, slot, env, h) for h in heads]
        return items

    @pl.when(t == 0)
    def _():
        for h in range(RET_HEADS):
            r = (h % 2) * RET_DK
            sret[h] = jnp.zeros((2 * RET_DK, RET_DV), F32)
            sret[h, r:r + RET_DK, :] = rs0_ref[0, h]
        for h in heads:
            sgdn[h] = gs0_ref[0, h]
        h0 = CONV_ROW0 - CONV_HIST
        xbuf[0:h0, :] = jnp.zeros((h0, GDN_CONV_CH), F32)
        xbuf[h0:CONV_ROW0, :] = cv0_ref[0]
        xn0 = _rms_rows(x0_ref[0], nw_ref[...]).astype(BF16)
        for c, cs in enumerate(col_blocks):
            proj_s[0, :, cs] = _dot(xn0, win_ref[c])
        for item in prework(0, False):
            item()

    def proj(off, width):
        return proj_s[cur, :, off:off + width]

    def project_next(c):
        proj_s[1 - cur, :, col_blocks[c]] = _dot(xn_s[...], win_ref[c])

    first_half = (lane % RET_DK) < (RET_DK // 2)

    def rope(x):
        rot = jnp.where(first_half, pltpu.roll(x, LANES - RET_DK // 2, 1), pltpu.roll(x, RET_DK // 2, 1))
        return x * cos_ref[...] + rot * sin_ref[...]

    def retention_pair(j):
        qp = rope(proj(OFF_RQ + LANES * j, LANES))
        kp = rope(proj(OFF_RK + LANES * j, LANES)) * (RET_DK ** -0.5)
        qpb = qp.astype(BF16)
        for e in range(2):
            h = 2 * j + e
            km = jnp.where(lane >= RET_DK if e else lane < RET_DK, kp, 0.0)
            v = proj(OFF_RV + RET_DV * h, RET_DV).astype(BF16)
            s_prev = sret[h]
            scores = _dot_nt(qpb, km.astype(BF16)) * dmat_ref[h]
            lhs = jnp.concatenate([scores.astype(BF16), (qp * qdec_ref[h]).astype(BF16)], axis=1)
            o = _dot(lhs, jnp.concatenate([v, s_prev.astype(BF16)], axis=0))
            sret[h] = gam_ref[h] * s_prev + _dot_tn((km * kdec_ref[h]).astype(BF16), v)
            mu = jnp.mean(o, axis=-1, keepdims=True)
            oc = o - mu
            var = jnp.mean(oc * oc, axis=-1, keepdims=True)
            y = oc * lax.rsqrt(var + EPS) * rgn_ref[:, RET_DV * h:RET_DV * (h + 1)]
            cat_ref[0, :, RET_DV * h:RET_DV * (h + 1)] = (y * _silu(proj(OFF_RG + RET_DV * h, RET_DV))).astype(BF16)

    def gmlp_group(g):
        cs = slice(GMLP_DIM * g, GMLP_DIM * (g + 1))
        v = proj(OFF_CV + GMLP_DIM * g, GMLP_DIM)
        mu = jnp.mean(v, axis=-1, keepdims=True)
        vc = v - mu
        var = jnp.mean(vc * vc, axis=-1, keepdims=True)
        vg = vc * lax.rsqrt(var + EPS) * mnw_ref[:, cs]
        if emit_vn:
            vn_ref[0, :, cs] = vg
        wtri = jnp.where(tri, ws_ref[g], 0.0).astype(BF16)
        z = _dot(wtri, vg.astype(BF16)) + bs_ref[g]
        u = proj(OFF_CU + GMLP_DIM * g, GMLP_DIM)
        cat_ref[0, :, RET_V + GDN_V + GMLP_DIM * g:RET_V + GDN_V + GMLP_DIM * (g + 1)] = (u * z).astype(BF16)

    side = [functools.partial(retention_pair, j) for j in range(RET_HEADS // 2)]
    side += [functools.partial(gmlp_group, g) for g in range(GMLP_GROUPS)]
    dense, ahead = [], []
    if pipelined:
        first = [c for c in range(len(col_blocks))
                 if OFF_GQKV <= PROJ_COLS * c < OFF_GQKV + GDN_CONV_CH or PROJ_COLS * c <= OFF_AB < PROJ_COLS * (c + 1)]
        order = first + [c for c in range(len(col_blocks)) if c not in first]
        dense = [functools.partial(project_next, c) for c in order]
        n_first = len(first)
        ahead = prework(1 - cur, True)

    issued = [0]

    def fill(n_dense, n_ahead=0, n_side=0):
        for _ in range(n_dense):
            if dense:
                dense.pop(0)()
                issued[0] += 1
        for _ in range(n_ahead):
            if ahead and issued[0] >= n_first:
                ahead.pop(0)()
        for _ in range(n_side):
            if side:
                side.pop(0)()

    if pipelined:
        xn_s[...] = _rms_rows(x1_ref[0], nw_ref[...]).astype(BF16)

    zero_b = jnp.zeros((tt, tt), BF16)

    def pair_dot(a2, b2):
        bb = b2.astype(BF16)
        rhs = jnp.concatenate([jnp.concatenate([bb[:, :tt], zero_b], axis=1),
                               jnp.concatenate([zero_b, bb[:, tt:]], axis=1)], axis=0)
        return _dot(a2.astype(BF16), rhs)

    def same_block(b):
        return (ri // b) == (ci // b)

    pairs = range(GDN_HEADS // 2)
    eye2 = jnp.concatenate([eye, eye], axis=1)
    l2 = [jnp.concatenate([l_s[cur, 2 * j], l_s[cur, 2 * j + 1]], axis=1) for j in pairs]
    diag = same_block(INV_BLOCK)
    diag2 = jnp.concatenate([diag, diag], axis=1)
    pw = [jnp.where(diag2, -l2[j], 0.0) for j in pairs]
    td = [eye2 + pw[j] for j in pairs]
    n_fac = max(int(math.ceil(math.log2(min(INV_BLOCK, valid)))), 1)
    for it in range(n_fac):
        if it == 0:
            if n_fac > 1:
                pw = [pair_dot(pw[j], pw[j]) for j in pairs]
        elif it == n_fac - 1:
            td = [td[j] + pair_dot(td[j], pw[j]) for j in pairs]
        else:
            r = [pair_dot(jnp.concatenate([pw[j], td[j]], axis=0), pw[j]) for j in pairs]
            pw = [r[j][:tt] for j in pairs]
            td = [td[j] + r[j][tt:] for j in pairs]
        fill(*FILL_SERIES)
    blk = INV_BLOCK
    while blk < valid:
        off = same_block(2 * blk) & jnp.logical_not(same_block(blk))
        off2 = jnp.concatenate([off, off], axis=1)
        y = [pair_dot(jnp.where(off2, l2[j], 0.0), td[j]) for j in pairs]
        fill(*FILL_MERGE)
        z = [pair_dot(td[j], y[j]) for j in pairs]
        td = [td[j] - z[j] for j in pairs]
        fill(*FILL_MERGE)
        blk *= 2

    gcum = gcum_s[cur]
    beta = beta_s[cur]
    gls = []
    for h in heads:
        j, e = divmod(h, 2)
        tinv = td[j][:, tt * e:tt * (e + 1)]
        gcol = gcum[:, h:h + 1]
        eg = jnp.exp(gcol)
        bcol = beta[:, GDN_HEADS + h:GDN_HEADS + h + 1]
        k = gk_s[cur, h]
        rhs = jnp.concatenate([bcol * gv_s[cur, h], (bcol * eg) * k], axis=1).astype(BF16)
        sol = _dot(tinv.astype(BF16), rhs)
        uv_s[h] = sol[:, :GDN_DV]
        a1_s[h, 0:tt, :] = sol[:, GDN_DV:].astype(BF16)
        a1_s[h, tt:2 * tt, :] = (gq_s[cur, h] * eg).astype(BF16)
        glast = gcum[tt - 1:tt, h:h + 1]
        a2_s[cur, h, tt:2 * tt, :] = (k * jnp.exp(glast - gcol)).T.astype(BF16)
        gls.append(jnp.exp(glast))
        fill(*FILL_SOLVE)

    ss = [sgdn[h] for h in heads]
    r1 = [_dot(a1_s[h], ss[h].astype(BF16)) for h in heads]
    fill(*FILL_RECUR)
    ub = [(uv_s[h] - r1[h][:tt]).astype(BF16) for h in heads]
    r2 = [_dot(a2_s[cur, h], ub[h]) for h in heads]
    fill(*FILL_RECUR)
    for h in heads:
        sgdn[h] = ss[h] * gls[h] + r2[h][tt:]
        o = r1[h][tt:] + r2[h][:tt]
        y = o * lax.rsqrt(jnp.mean(o * o, axis=-1, keepdims=True) + EPS) * gnw_ref[...]
        cat_ref[0, :, RET_V + GDN_DV * h:RET_V + GDN_DV * (h + 1)] = (y * _silu(proj(OFF_GZ + GDN_DV * h, GDN_DV))).astype(BF16)

    fill(len(dense), 0, 0)
    fill(0, len(ahead), len(side))

    @pl.when(t == nt - 1)
    def _():
        for h in range(RET_HEADS):
            r = (h % 2) * RET_DK
            rs_ref[0, h] = sret[h, r:r + RET_DK, :]
        for h in heads:
            gs_ref[0, h] = sgdn[h]


def _mixer(x, nw, w_in, l, cosf, sinf, consts, lp, rs0, gs0, cv0, valid, emit_vn):
    b, t, d = x.shape
    tt = TILE
    nt = t // tt
    dmat, qdec, kdec, gam = consts
    full = lambda a: pl.BlockSpec(a.shape, lambda i, j, _n=a.ndim: (0,) * _n)
    small = [cosf, sinf, dmat, qdec, kdec, gam,
             lp["alog"], lp["dtb"], lp["convw"], lp["rgn"], lp["gnw"], lp["mnw"], lp["ws"], lp["bs"]]
    state_specs = [
        pl.BlockSpec((1, RET_HEADS, RET_DK, RET_DV), lambda i, j: (i, 0, 0, 0)),
        pl.BlockSpec((1, GDN_HEADS, GDN_DK, GDN_DV), lambda i, j: (i, 0, 0, 0)),
        pl.BlockSpec((1, CONV_WIDTH - 1, GDN_CONV_CH), lambda i, j: (i, 0, 0)),
    ]
    in_specs = [
        pl.BlockSpec((1, tt, d), lambda i, j: (i, 0, 0)),
        pl.BlockSpec((1, tt, d), lambda i, j: (i, jnp.minimum(j + 1, nt - 1), 0)),
        pl.BlockSpec((1, d), lambda i, j: (0, 0)),
        pl.BlockSpec((None,) + w_in.shape[1:], lambda i, j: (l, 0, 0, 0), pipeline_mode=pl.Buffered(1)),
        pl.BlockSpec((tt, LANES), lambda i, j: (j, 0)),
        pl.BlockSpec((tt, LANES), lambda i, j: (j, 0)),
    ] + [full(a) for a in small[2:]] + state_specs
    out_shape = [
        jax.ShapeDtypeStruct((b, t, D_MIX), BF16),
        jax.ShapeDtypeStruct((b, RET_HEADS, RET_DK, RET_DV), F32),
        jax.ShapeDtypeStruct((b, GDN_HEADS, GDN_DK, GDN_DV), F32),
        jax.ShapeDtypeStruct((b, CONV_WIDTH - 1, GDN_CONV_CH), F32),
    ]
    out_specs = [pl.BlockSpec((1, tt, D_MIX), lambda i, j: (i, j, 0))] + state_specs
    if emit_vn:
        out_shape.append(jax.ShapeDtypeStruct((b, t, GMLP_W), F32))
        out_specs.append(pl.BlockSpec((1, tt, GMLP_W), lambda i, j: (i, j, 0)))
    hs = (GDN_HEADS, tt, LANES)
    scratch = [
        pltpu.VMEM((2, tt, D_PROJ), F32),
        pltpu.VMEM((tt, d), BF16),
        pltpu.VMEM((RET_HEADS, 2 * RET_DK, RET_DV), F32),
        pltpu.VMEM((GDN_HEADS, GDN_DK, GDN_DV), F32),
        pltpu.VMEM((CONV_ROW0 + tt, GDN_CONV_CH), F32),
        pltpu.VMEM((2,) + hs, F32), pltpu.VMEM((2,) + hs, F32), pltpu.VMEM((2,) + hs, F32),
        pltpu.VMEM((2, GDN_HEADS, tt, tt), F32),
        pltpu.VMEM((2, tt, LANES), F32), pltpu.VMEM((2, tt, LANES), F32),
        pltpu.VMEM(hs, F32),
        pltpu.VMEM((GDN_HEADS, 2 * tt, LANES), BF16),
        pltpu.VMEM((2, GDN_HEADS, 2 * tt, LANES), BF16),
    ]
    return pl.pallas_call(
        functools.partial(_mixer_kernel, valid, emit_vn, nt > 1),
        grid=(b, nt),
        in_specs=in_specs,
        out_specs=out_specs,
        out_shape=out_shape,
        scratch_shapes=scratch,
        compiler_params=_cparams(("parallel", "arbitrary")),
        name="mixer",
    )(x, x, nw, w_in, *small, rs0, gs0, cv0)


def _pad_rows(a, rows, axis):
    pad = [(0, 0)] * a.ndim
    pad[axis] = (0, rows - a.shape[axis])
    return jnp.pad(a, pad)


def _rope_tables(pos):
    half = RET_DK // 2
    inv = jnp.power(ROPE_BASE, -jnp.arange(half, dtype=F32) / half)
    ang = pos.astype(F32)[:, None] * inv[None, :]
    cos, sin = jnp.cos(ang), jnp.sin(ang)
    reps = LANES // RET_DK
    return jnp.tile(jnp.concatenate([cos, cos], axis=1), (1, reps)), jnp.tile(jnp.concatenate([-sin, sin], axis=1), (1, reps))


def _ret_consts(cr):
    log_gamma = jnp.log1p(-jnp.power(2.0, -5.0 - jnp.arange(RET_HEADS, dtype=F32)))
    idx = jnp.arange(cr, dtype=F32)
    diff = idx[:, None] - idx[None, :]
    causal = diff >= 0
    dmat = jnp.where(causal[None], jnp.exp(jnp.where(causal, diff, 0.0)[None] * log_gamma[:, None, None]), 0.0)
    kdec = jnp.exp(log_gamma[:, None] * (cr - 1 - idx)[None, :])[:, :, None]
    qdec = jnp.exp(log_gamma[:, None] * (idx + 1.0)[None, :])[:, :, None]
    gam = jnp.broadcast_to(jnp.exp(cr * log_gamma)[:, None, None], (RET_HEADS, 1, LANES))
    dmat = _pad_rows(_pad_rows(dmat, TILE, 1), TILE, 2)
    return dmat, _pad_rows(qdec, TILE, 1), _pad_rows(kdec, TILE, 1), gam


def _pad_lanes(v):
    return jnp.pad(v, (0, LANES - v.shape[0]))[None, :]


def _tiles(n):
    return dict(tm_ffn=min(1024, n), tf=512, tm_out=min(512, n))


def _trunk(x, pos, rs0, gs0, cv0, p, emit_vn):
    b, t, d = x.shape
    n = b * t
    tl = _tiles(n)
    valid = min(t, TILE)
    tpad = -(-t // TILE) * TILE
    cosf, sinf = _rope_tables(pos)
    cosf, sinf = _pad_rows(cosf, tpad, 0), _pad_rows(sinf, tpad, 0)
    consts = _ret_consts(valid)
    x = x.reshape(n, d)
    outs = []
    for l in range(DEPTH):
        x = _ffn(x, p["norm_ffn1_w"][l][None], p["ffn1_w_up"], p["ffn1_w_down"], l, tl["tm_ffn"], tl["tf"])
        lp = {
            "alog": _pad_lanes(p["gdn_a_log"][l]), "dtb": _pad_lanes(p["gdn_dt_bias"][l]),
            "convw": p["gdn_conv_w"][l], "rgn": p["ret_gn_w"][l][None], "gnw": p["gdn_norm_w"][l][None],
            "mnw": p["gmlp_norm_w"][l][None],
            "ws": _pad_rows(_pad_rows(p["gmlp_ws"][l][:, :valid, :valid], TILE, 1), TILE, 2),
            "bs": _pad_rows(p["gmlp_bs"][l][:, :valid, None], TILE, 1),
        }
        res = _mixer(_pad_rows(x.reshape(b, t, d), tpad, 1), p["norm_mix_w"][l][None], p["w_in"], l,
                     cosf, sinf, consts, lp, rs0[l], gs0[l], cv0[l], valid, emit_vn)
        cat = res[0][:, :t].reshape(n, D_MIX)
        x = _outproj(x, cat, p["w_out"], l, tl["tm_out"])
        x = _ffn(x, p["norm_ffn2_w"][l][None], p["ffn2_w_up"], p["ffn2_w_down"], l, tl["tm_ffn"], tl["tf"],
                 final_nw=p["final_norm_w"][None] if l == DEPTH - 1 else None)
        outs.append(tuple(res[1:4]) + ((res[4][:, :t],) if emit_vn else ()))
    y = x.reshape(b, t, d)
    return (y,) + tuple(jnp.stack([o[i] for o in outs]) for i in range(len(outs[0])))


def _win_kernel(w_ref, o_ref, flat_ref):
    ab0 = OFF_CU
    nab = 2 * GDN_HEADS
    flat_ref[:, :ab0] = w_ref[:, :ab0].astype(BF16)
    flat_ref[:, OFF_CU:OFF_AB] = w_ref[:, ab0 + nab:].astype(BF16)
    flat_ref[:, OFF_AB:] = jnp.zeros((flat_ref.shape[0], D_PROJ - OFF_AB), BF16)
    flat_ref[:, OFF_AB:OFF_AB + nab] = w_ref[:, ab0:ab0 + nab].astype(BF16)
    for c in range(D_PROJ // PROJ_COLS):
        o_ref[c] = flat_ref[:, PROJ_COLS * c:PROJ_COLS * (c + 1)]


def _relayout_w_in(w, tr=256):
    depth, d, d_in = w.shape
    return pl.pallas_call(
        _win_kernel,
        grid=(depth, d // tr),
        in_specs=[pl.BlockSpec((None, tr, d_in), lambda l, i: (l, i, 0))],
        out_specs=pl.BlockSpec((None, D_PROJ // PROJ_COLS, tr, PROJ_COLS), lambda l, i: (l, 0, i, 0)),
        out_shape=jax.ShapeDtypeStruct((depth, D_PROJ // PROJ_COLS, d, PROJ_COLS), BF16),
        scratch_shapes=[pltpu.VMEM((tr, D_PROJ), BF16)],
        compiler_params=_cparams(("parallel", "parallel")),
        name="w_in_layout",
    )(w)


def _prep_params(p):
    q = dict(p)
    for k in ("ffn1_w_up", "ffn1_w_down", "ffn2_w_up", "ffn2_w_down", "w_out"):
        q[k] = p[k].astype(BF16)
    q["w_in"] = _relayout_w_in(p["w_in"])
    return q


def kernel(x_prompt, x_sample, state_ret, state_gdn, state_conv, norm_ffn1_w, ffn1_w_up, ffn1_w_down,
           norm_mix_w, w_in, ret_gn_w, gdn_conv_w, gdn_a_log, gdn_dt_bias, gdn_norm_w, gmlp_ws, gmlp_bs,
           gmlp_norm_w, w_out, norm_ffn2_w, ffn2_w_up, ffn2_w_down, final_norm_w):
    p = _prep_params({
        "norm_ffn1_w": norm_ffn1_w, "ffn1_w_up": ffn1_w_up, "ffn1_w_down": ffn1_w_down,
        "norm_mix_w": norm_mix_w, "w_in": w_in, "ret_gn_w": ret_gn_w, "gdn_conv_w": gdn_conv_w,
        "gdn_a_log": gdn_a_log, "gdn_dt_bias": gdn_dt_bias, "gdn_norm_w": gdn_norm_w,
        "gmlp_ws": gmlp_ws, "gmlp_bs": gmlp_bs, "gmlp_norm_w": gmlp_norm_w, "w_out": w_out,
        "norm_ffn2_w": norm_ffn2_w, "ffn2_w_up": ffn2_w_up, "ffn2_w_down": ffn2_w_down,
        "final_norm_w": final_norm_w})
    bp, tp = x_prompt.shape[0], x_prompt.shape[1]
    bs, ts = x_sample.shape[0], x_sample.shape[1]
    ret0 = jnp.zeros((DEPTH, bp, RET_HEADS, RET_DK, RET_DV), F32)
    gdn0 = jnp.zeros((DEPTH, bp, GDN_HEADS, GDN_DK, GDN_DV), F32)
    conv0 = jnp.zeros((DEPTH, bp, CONV_WIDTH - 1, GDN_CONV_CH), F32)
    pos_p = jnp.arange(tp, dtype=jnp.int32)
    y_p, ret_p, gdn_p, conv_p = _trunk(x_prompt, pos_p, ret0, gdn0, conv0, p, emit_vn=False)
    pos_s = PAST_LEN + jnp.arange(ts, dtype=jnp.int32)
    y_s, ret_s, gdn_s, conv_s, vn_s = _trunk(x_sample, pos_s, state_ret, state_gdn, state_conv, p, emit_vn=True)
    return (y_p, y_s, ret_p, ret_s, gdn_p, gdn_s, conv_p, conv_s, vn_s)
```

```python
import functools
import math

import jax
import jax.numpy as jnp
from jax import lax
from jax.experimental import pallas as pl
from jax.experimental.pallas import tpu as pltpu

F32 = jnp.float32
BF16 = jnp.bfloat16

DEPTH = 4
PAST_LEN = 4096
EPS = 1e-6
ROPE_BASE = 10000.0
RET_HEADS = 6
RET_DK = 64
RET_DV = 128
GDN_HEADS = 6
GDN_DK = 128
GDN_DV = 128
CONV_WIDTH = 4
GMLP_GROUPS = 4
GMLP_DIM = 128

RET_QK = RET_HEADS * RET_DK
RET_V = RET_HEADS * RET_DV
GDN_QK = GDN_HEADS * GDN_DK
GDN_V = GDN_HEADS * GDN_DV
GDN_CONV_CH = 2 * GDN_QK + GDN_V
GMLP_W = GMLP_GROUPS * GMLP_DIM
D_MIX = RET_V + GDN_V + GMLP_W

LANES = 128
SUBLANES = 8
CONV_HIST = CONV_WIDTH - 1
CONV_ROW0 = SUBLANES
TILE = 128
INV_BLOCK = 16
PROJ_COLS = 256
FILL_SERIES, FILL_MERGE, FILL_SOLVE, FILL_RECUR = (3, 0, 1), (2, 3, 1), (0, 1, 0), (1, 1, 0)

OFF_RQ = 0
OFF_RK = OFF_RQ + RET_QK
OFF_RV = OFF_RK + RET_QK
OFF_RG = OFF_RV + RET_V
OFF_GQKV = OFF_RG + RET_V
OFF_GZ = OFF_GQKV + GDN_CONV_CH
OFF_CU = OFF_GZ + GDN_V
OFF_CV = OFF_CU + GMLP_W
OFF_AB = OFF_CV + GMLP_W
D_PROJ = 6656

VMEM_LIMIT = 56 * 1024 * 1024


def _cparams(sem):
    return pltpu.CompilerParams(dimension_semantics=sem, vmem_limit_bytes=VMEM_LIMIT)


def _dot(a, b):
    return jnp.dot(a, b, preferred_element_type=F32)


def _dot_nt(a, b):
    return lax.dot_general(a, b, (((1,), (1,)), ((), ())), preferred_element_type=F32)


def _dot_tn(a, b):
    return lax.dot_general(a, b, (((0,), (0,)), ((), ())), preferred_element_type=F32)


def _silu(x):
    return x * jax.nn.sigmoid(x)


def _rms_rows(x, w):
    return x * lax.rsqrt(jnp.mean(x * x, axis=-1, keepdims=True) + EPS) * w


def _ffn_kernel(x_ref, nw_ref, wg_ref, wu_ref, wd_ref, *rest):
    fnw_ref = rest[0] if len(rest) == 3 else None
    o_ref, xn_ref = rest[-2:]
    j = pl.program_id(1)

    def delta():
        xn = xn_ref[...]
        g = _dot(xn, wg_ref[...])
        u = _dot(xn, wu_ref[...])
        a = (0.5 * _silu(g) * u).astype(BF16)
        return _dot(a, wd_ref[...])

    @pl.when(j == 0)
    def _():
        xn_ref[...] = _rms_rows(x_ref[...], nw_ref[...]).astype(BF16)
        o_ref[...] = x_ref[...] + delta()

    @pl.when(j != 0)
    def _():
        o_ref[...] += delta()

    if fnw_ref is not None:
        @pl.when(j == pl.num_programs(1) - 1)
        def _():
            o_ref[...] = _rms_rows(o_ref[...], fnw_ref[...])


def _ffn(x, nw, w_up, w_down, l, tm, tf, final_nw=None):
    n, d = x.shape
    dff = w_down.shape[1]
    nf = dff // tf
    vec = pl.BlockSpec((1, d), lambda i, j: (0, 0))
    return pl.pallas_call(
        _ffn_kernel,
        grid=(n // tm, nf),
        in_specs=[
            pl.BlockSpec((tm, d), lambda i, j: (i, 0)),
            vec,
            pl.BlockSpec((None, d, tf), lambda i, j: (l, 0, j)),
            pl.BlockSpec((None, d, tf), lambda i, j: (l, 0, j + nf)),
            pl.BlockSpec((None, tf, d), lambda i, j: (l, j, 0)),
        ] + ([] if final_nw is None else [vec]),
        out_specs=pl.BlockSpec((tm, d), lambda i, j: (i, 0)),
        out_shape=jax.ShapeDtypeStruct((n, d), F32),
        scratch_shapes=[pltpu.VMEM((tm, d), BF16)],
        compiler_params=_cparams(("parallel", "arbitrary")),
        name="ffn",
    )(x, nw, w_up, w_up, w_down, *(() if final_nw is None else (final_nw,)))


def _outproj_kernel(x_ref, c_ref, w_ref, o_ref):
    o_ref[...] = x_ref[...] + _dot(c_ref[...], w_ref[...])


def _outproj(x, cat, w, l, tm):
    n, d = x.shape
    k = cat.shape[1]
    return pl.pallas_call(
        _outproj_kernel,
        grid=(n // tm,),
        in_specs=[
            pl.BlockSpec((tm, d), lambda i: (i, 0)),
            pl.BlockSpec((tm, k), lambda i: (i, 0)),
            pl.BlockSpec((None, k, d), lambda i: (l, 0, 0)),
        ],
        out_specs=pl.BlockSpec((tm, d), lambda i: (i, 0)),
        out_shape=jax.ShapeDtypeStruct((n, d), F32),
        compiler_params=_cparams(("parallel",)),
        name="outproj",
    )(x, cat, w)


def _split3(x):
    hi = x.astype(BF16)
    r1 = x - hi.astype(F32)
    mid = r1.astype(BF16)
    lo = (r1 - mid.astype(F32)).astype(BF16)
    return hi, mid, lo


def _mixer_kernel(valid, emit_vn, pipelined, layer,
                  x0_ref, x1_ref, nw_ref, win_hbm,
                  cos_ref, sin_ref, dmat_ref, qdec_ref, kdec_ref, gam_ref,
                  alog_ref, dtb_ref, convw_ref, rgn_ref, gnw_ref, mnw_ref, ws_ref, bs_ref,
                  rs0_ref, gs0_ref, cv0_ref,
                  cat_ref, rs_ref, gs_ref, cvn_ref, *rest):
    if emit_vn:
        vn_ref = rest[0]
        rest = rest[1:]
    else:
        vn_ref = None
    win_s, win_sem, proj_s, xn_s, sret, sgdn, xbuf, gq_s, gk_s, gv_s, l_s, gcum_s, beta_s, uv_s, a1_s, a2_s = rest
    tt = TILE
    t = pl.program_id(1)
    nt = pl.num_programs(1)
    cur = t % 2
    heads = range(GDN_HEADS)
    col_blocks = [slice(PROJ_COLS * c, PROJ_COLS * (c + 1)) for c in range(D_PROJ // PROJ_COLS)]

    lane = lax.broadcasted_iota(jnp.int32, (tt, LANES), 1)
    ri = lax.broadcasted_iota(jnp.int32, (tt, tt), 0)
    ci = lax.broadcasted_iota(jnp.int32, (tt, tt), 1)
    tri = ri >= ci
    strict = ri > ci
    eye = jnp.where(ri == ci, 1.0, 0.0).astype(F32)

    def conv_load(slot, carry):
        h0 = CONV_ROW0 - CONV_HIST
        if carry:
            xbuf[h0:CONV_ROW0, :] = xbuf[h0 + tt:CONV_ROW0 + tt, :]
        xbuf[CONV_ROW0:CONV_ROW0 + tt, :] = proj_s[slot, :, OFF_GQKV:OFF_GQKV + GDN_CONV_CH]
        cvn_ref[0] = xbuf[h0 + valid:CONV_ROW0 + valid, :]

    def conv_piece(slot, p):
        cs = slice(LANES * p, LANES * (p + 1))
        xc = xbuf[:, cs]
        y = xc * convw_ref[0:1, cs]
        for i in range(1, CONV_WIDTH):
            y = pltpu.roll(y, 1, 0) + xc * convw_ref[i:i + 1, cs]
        y = _silu(y[CONV_ROW0:CONV_ROW0 + tt])
        kind, h = divmod(p, GDN_HEADS)
        if kind == 2:
            gv_s[slot, h] = y
        else:
            yn = y * lax.rsqrt(jnp.sum(y * y, axis=-1, keepdims=True) + EPS)
            if kind == 0:
                gq_s[slot, h] = yn * (GDN_DK ** -0.5)
            else:
                gk_s[slot, h] = yn

    def gates(slot, env):
        ab = proj_s[slot, :, OFF_AB:OFF_AB + LANES]
        log_a = -jnp.exp(alog_ref[...]) * jax.nn.softplus(ab + dtb_ref[...])
        beta = jax.nn.sigmoid(ab)
        if valid < tt:
            real = lax.broadcasted_iota(jnp.int32, (tt, LANES), 0) < valid
            log_a = jnp.where(real, log_a, 0.0)
            beta = jnp.where(real, beta, 0.0)
        cum_mat = jnp.where(tri, 1.0, 0.0).astype(BF16)
        la_hi, la_mid, la_lo = _split3(log_a)
        gcum = (_dot(cum_mat, la_hi) + _dot(cum_mat, la_mid)) + _dot(cum_mat, la_lo)
        gcum_s[slot] = gcum
        beta_s[slot] = beta
        env["gcum"], env["gcum_t"], env["beta"] = gcum, gcum.T, beta

    def system_matrix(slot, env, h):
        gcol = env["gcum"][:, h:h + 1]
        grow = env["gcum_t"][h:h + 1, :]
        decay = jnp.exp(jnp.where(tri, gcol - grow, -1e30))
        bcol = env["beta"][:, GDN_HEADS + h:GDN_HEADS + h + 1]
        kb = gk_s[slot, h].astype(BF16)
        r = _dot_nt(jnp.concatenate([gq_s[slot, h].astype(BF16), kb], axis=0), kb)
        a2_s[slot, h, 0:tt, :] = (r[:tt] * decay).astype(BF16)
        l_s[slot, h] = bcol * r[tt:] * jnp.where(strict, decay, 0.0)

    def prework(slot, carry):
        env = {}
        items = [functools.partial(conv_load, slot, carry)]
        items += [functools.partial(conv_piece, slot, p) for p in range(GDN_CONV_CH // LANES)]
        items += [functools.partial(gates, slot, env)]
        items += [functools.partial(system_matrix, slot, env, h) for h in heads]
        return items

    @pl.when((pl.program_id(0) == 0) & (t == 0))
    def _():
        copy = pltpu.make_async_copy(win_hbm.at[layer], win_s, win_sem)
        copy.start()
        copy.wait()

    @pl.when(t == 0)
    def _():
        for h in range(RET_HEADS):
            r = (h % 2) * RET_DK
            sret[h] = jnp.zeros((2 * RET_DK, RET_DV), F32)
            sret[h, r:r + RET_DK, :] = rs0_ref[0, h]
        for h in heads:
            sgdn[h] = gs0_ref[0, h]
        h0 = CONV_ROW0 - CONV_HIST
        xbuf[0:h0, :] = jnp.zeros((h0, GDN_CONV_CH), F32)
        xbuf[h0:CONV_ROW0, :] = cv0_ref[0]
        xn0 = _rms_rows(x0_ref[0], nw_ref[...]).astype(BF16)
        for c, cs in enumerate(col_blocks):
            proj_s[0, :, cs] = _dot(xn0, win_s[c])
        for item in prework(0, False):
            item()

    def proj(off, width):
        return proj_s[cur, :, off:off + width]

    def project_next(c):
        proj_s[1 - cur, :, col_blocks[c]] = _dot(xn_s[...], win_s[c])

    first_half = (lane % RET_DK) < (RET_DK // 2)

    def rope(x):
        rot = jnp.where(first_half, pltpu.roll(x, LANES - RET_DK // 2, 1), pltpu.roll(x, RET_DK // 2, 1))
        return x * cos_ref[...] + rot * sin_ref[...]

    def retention_pair(j):
        qp = rope(proj(OFF_RQ + LANES * j, LANES))
        kp = rope(proj(OFF_RK + LANES * j, LANES)) * (RET_DK ** -0.5)
        qpb = qp.astype(BF16)
        for e in range(2):
            h = 2 * j + e
            km = jnp.where(lane >= RET_DK if e else lane < RET_DK, kp, 0.0)
            v = proj(OFF_RV + RET_DV * h, RET_DV).astype(BF16)
            s_prev = sret[h]
            scores = _dot_nt(qpb, km.astype(BF16)) * dmat_ref[h]
            lhs = jnp.concatenate([scores.astype(BF16), (qp * qdec_ref[h]).astype(BF16)], axis=1)
            o = _dot(lhs, jnp.concatenate([v, s_prev.astype(BF16)], axis=0))
            sret[h] = gam_ref[h] * s_prev + _dot_tn((km * kdec_ref[h]).astype(BF16), v)
            mu = jnp.mean(o, axis=-1, keepdims=True)
            oc = o - mu
            var = jnp.mean(oc * oc, axis=-1, keepdims=True)
            y = oc * lax.rsqrt(var + EPS) * rgn_ref[:, RET_DV * h:RET_DV * (h + 1)]
            cat_ref[0, :, RET_DV * h:RET_DV * (h + 1)] = (y * _silu(proj(OFF_RG + RET_DV * h, RET_DV))).astype(BF16)

    def gmlp_group(g):
        cs = slice(GMLP_DIM * g, GMLP_DIM * (g + 1))
        v = proj(OFF_CV + GMLP_DIM * g, GMLP_DIM)
        mu = jnp.mean(v, axis=-1, keepdims=True)
        vc = v - mu
        var = jnp.mean(vc * vc, axis=-1, keepdims=True)
        vg = vc * lax.rsqrt(var + EPS) * mnw_ref[:, cs]
        if emit_vn:
            vn_ref[0, :, cs] = vg
        wtri = jnp.where(tri, ws_ref[g], 0.0).astype(BF16)
        z = _dot(wtri, vg.astype(BF16)) + bs_ref[g]
        u = proj(OFF_CU + GMLP_DIM * g, GMLP_DIM)
        cat_ref[0, :, RET_V + GDN_V + GMLP_DIM * g:RET_V + GDN_V + GMLP_DIM * (g + 1)] = (u * z).astype(BF16)

    side = [functools.partial(retention_pair, j) for j in range(RET_HEADS // 2)]
    side += [functools.partial(gmlp_group, g) for g in range(GMLP_GROUPS)]
    dense, ahead = [], []
    if pipelined:
        first = [c for c in range(len(col_blocks))
                 if OFF_GQKV <= PROJ_COLS * c < OFF_GQKV + GDN_CONV_CH or PROJ_COLS * c <= OFF_AB < PROJ_COLS * (c + 1)]
        order = first + [c for c in range(len(col_blocks)) if c not in first]
        dense = [functools.partial(project_next, c) for c in order]
        n_first = len(first)
        ahead = prework(1 - cur, True)

    issued = [0]

    def fill(n_dense, n_ahead=0, n_side=0):
        for _ in range(n_dense):
            if dense:
                dense.pop(0)()
                issued[0] += 1
        for _ in range(n_ahead):
            if ahead and issued[0] >= n_first:
                ahead.pop(0)()
        for _ in range(n_side):
            if side:
                side.pop(0)()

    if pipelined:
        xn_s[...] = _rms_rows(x1_ref[0], nw_ref[...]).astype(BF16)

    zero_b = jnp.zeros((tt, tt), BF16)

    def pair_dot(a2, b2):
        bb = b2.astype(BF16)
        rhs = jnp.concatenate([jnp.concatenate([bb[:, :tt], zero_b], axis=1),
                               jnp.concatenate([zero_b, bb[:, tt:]], axis=1)], axis=0)
        return _dot(a2.astype(BF16), rhs)

    def same_block(b):
        return (ri // b) == (ci // b)

    pairs = range(GDN_HEADS // 2)
    eye2 = jnp.concatenate([eye, eye], axis=1)
    l2 = [jnp.concatenate([l_s[cur, 2 * j], l_s[cur, 2 * j + 1]], axis=1) for j in pairs]
    diag = same_block(INV_BLOCK)
    diag2 = jnp.concatenate([diag, diag], axis=1)
    pw = [jnp.where(diag2, -l2[j], 0.0) for j in pairs]
    td = [eye2 + pw[j] for j in pairs]
    n_fac = max(int(math.ceil(math.log2(min(INV_BLOCK, valid)))), 1)
    for it in range(n_fac):
        if it == 0:
            if n_fac > 1:
                pw = [pair_dot(pw[j], pw[j]) for j in pairs]
        elif it == n_fac - 1:
            td = [td[j] + pair_dot(td[j], pw[j]) for j in pairs]
        else:
            r = [pair_dot(jnp.concatenate([pw[j], td[j]], axis=0), pw[j]) for j in pairs]
            pw = [r[j][:tt] for j in pairs]
            td = [td[j] + r[j][tt:] for j in pairs]
        fill(*FILL_SERIES)
    blk = INV_BLOCK
    while blk < valid:
        off = same_block(2 * blk) & jnp.logical_not(same_block(blk))
        off2 = jnp.concatenate([off, off], axis=1)
        y = [pair_dot(jnp.where(off2, l2[j], 0.0), td[j]) for j in pairs]
        fill(*FILL_MERGE)
        z = [pair_dot(td[j], y[j]) for j in pairs]
        td = [td[j] - z[j] for j in pairs]
        fill(*FILL_MERGE)
        blk *= 2

    gcum = gcum_s[cur]
    beta = beta_s[cur]
    gls = []
    for h in heads:
        j, e = divmod(h, 2)
        tinv = td[j][:, tt * e:tt * (e + 1)]
        gcol = gcum[:, h:h + 1]
        eg = jnp.exp(gcol)
        bcol = beta[:, GDN_HEADS + h:GDN_HEADS + h + 1]
        k = gk_s[cur, h]
        rhs = jnp.concatenate([bcol * gv_s[cur, h], (bcol * eg) * k], axis=1).astype(BF16)
        sol = _dot(tinv.astype(BF16), rhs)
        uv_s[h] = sol[:, :GDN_DV]
        a1_s[h, 0:tt, :] = sol[:, GDN_DV:].astype(BF16)
        a1_s[h, tt:2 * tt, :] = (gq_s[cur, h] * eg).astype(BF16)
        glast = gcum[tt - 1:tt, h:h + 1]
        a2_s[cur, h, tt:2 * tt, :] = (k * jnp.exp(glast - gcol)).T.astype(BF16)
        gls.append(jnp.exp(glast))
        fill(*FILL_SOLVE)

    ss = [sgdn[h] for h in heads]
    r1 = [_dot(a1_s[h], ss[h].astype(BF16)) for h in heads]
    fill(*FILL_RECUR)
    ub = [(uv_s[h] - r1[h][:tt]).astype(BF16) for h in heads]
    r2 = [_dot(a2_s[cur, h], ub[h]) for h in heads]
    fill(*FILL_RECUR)
    for h in heads:
        sgdn[h] = ss[h] * gls[h] + r2[h][tt:]
        o = r1[h][tt:] + r2[h][:tt]
        y = o * lax.rsqrt(jnp.mean(o * o, axis=-1, keepdims=True) + EPS) * gnw_ref[...]
        cat_ref[0, :, RET_V + GDN_DV * h:RET_V + GDN_DV * (h + 1)] = (y * _silu(proj(OFF_GZ + GDN_DV * h, GDN_DV))).astype(BF16)

    fill(len(dense), 0, 0)
    fill(0, len(ahead), len(side))

    @pl.when(t == nt - 1)
    def _():
        for h in range(RET_HEADS):
            r = (h % 2) * RET_DK
            rs_ref[0, h] = sret[h, r:r + RET_DK, :]
        for h in heads:
            gs_ref[0, h] = sgdn[h]


def _mixer(x, nw, w_in, l, cosf, sinf, consts, lp, rs0, gs0, cv0, valid, emit_vn):
    b, t, d = x.shape
    tt = TILE
    nt = t // tt
    dmat, qdec, kdec, gam = consts
    full = lambda a: pl.BlockSpec(a.shape, lambda i, j, _n=a.ndim: (0,) * _n)
    small = [cosf, sinf, dmat, qdec, kdec, gam,
             lp["alog"], lp["dtb"], lp["convw"], lp["rgn"], lp["gnw"], lp["mnw"], lp["ws"], lp["bs"]]
    state_specs = [
        pl.BlockSpec((1, RET_HEADS, RET_DK, RET_DV), lambda i, j: (i, 0, 0, 0)),
        pl.BlockSpec((1, GDN_HEADS, GDN_DK, GDN_DV), lambda i, j: (i, 0, 0, 0)),
        pl.BlockSpec((1, CONV_WIDTH - 1, GDN_CONV_CH), lambda i, j: (i, 0, 0)),
    ]
    in_specs = [
        pl.BlockSpec((1, tt, d), lambda i, j: (i, 0, 0)),
        pl.BlockSpec((1, tt, d), lambda i, j: (i, jnp.minimum(j + 1, nt - 1), 0)),
        pl.BlockSpec((1, d), lambda i, j: (0, 0)),
        pl.BlockSpec(memory_space=pl.ANY),
        pl.BlockSpec((tt, LANES), lambda i, j: (j, 0)),
        pl.BlockSpec((tt, LANES), lambda i, j: (j, 0)),
    ] + [full(a) for a in small[2:]] + state_specs
    out_shape = [
        jax.ShapeDtypeStruct((b, t, D_MIX), BF16),
        jax.ShapeDtypeStruct((b, RET_HEADS, RET_DK, RET_DV), F32),
        jax.ShapeDtypeStruct((b, GDN_HEADS, GDN_DK, GDN_DV), F32),
        jax.ShapeDtypeStruct((b, CONV_WIDTH - 1, GDN_CONV_CH), F32),
    ]
    out_specs = [pl.BlockSpec((1, tt, D_MIX), lambda i, j: (i, j, 0))] + state_specs
    if emit_vn:
        out_shape.append(jax.ShapeDtypeStruct((b, t, GMLP_W), F32))
        out_specs.append(pl.BlockSpec((1, tt, GMLP_W), lambda i, j: (i, j, 0)))
    hs = (GDN_HEADS, tt, LANES)
    scratch = [
        pltpu.VMEM(w_in.shape[1:], BF16), pltpu.SemaphoreType.DMA(()),
        pltpu.VMEM((2, tt, D_PROJ), F32),
        pltpu.VMEM((tt, d), BF16),
        pltpu.VMEM((RET_HEADS, 2 * RET_DK, RET_DV), F32),
        pltpu.VMEM((GDN_HEADS, GDN_DK, GDN_DV), F32),
        pltpu.VMEM((CONV_ROW0 + tt, GDN_CONV_CH), F32),
        pltpu.VMEM((2,) + hs, F32), pltpu.VMEM((2,) + hs, F32), pltpu.VMEM((2,) + hs, F32),
        pltpu.VMEM((2, GDN_HEADS, tt, tt), F32),
        pltpu.VMEM((2, tt, LANES), F32), pltpu.VMEM((2, tt, LANES), F32),
        pltpu.VMEM(hs, F32),
        pltpu.VMEM((GDN_HEADS, 2 * tt, LANES), BF16),
        pltpu.VMEM((2, GDN_HEADS, 2 * tt, LANES), BF16),
    ]
    return pl.pallas_call(
        functools.partial(_mixer_kernel, valid, emit_vn, nt > 1, l),
        grid=(b, nt),
        in_specs=in_specs,
        out_specs=out_specs,
        out_shape=out_shape,
        scratch_shapes=scratch,
        compiler_params=_cparams(("arbitrary", "arbitrary")),
        name="mixer",
    )(x, x, nw, w_in, *small, rs0, gs0, cv0)


def _pad_rows(a, rows, axis):
    pad = [(0, 0)] * a.ndim
    pad[axis] = (0, rows - a.shape[axis])
    return jnp.pad(a, pad)


def _rope_tables(pos):
    half = RET_DK // 2
    inv = jnp.power(ROPE_BASE, -jnp.arange(half, dtype=F32) / half)
    ang = pos.astype(F32)[:, None] * inv[None, :]
    cos, sin = jnp.cos(ang), jnp.sin(ang)
    reps = LANES // RET_DK
    return jnp.tile(jnp.concatenate([cos, cos], axis=1), (1, reps)), jnp.tile(jnp.concatenate([-sin, sin], axis=1), (1, reps))


def _ret_consts(cr):
    log_gamma = jnp.log1p(-jnp.power(2.0, -5.0 - jnp.arange(RET_HEADS, dtype=F32)))
    idx = jnp.arange(cr, dtype=F32)
    diff = idx[:, None] - idx[None, :]
    causal = diff >= 0
    dmat = jnp.where(causal[None], jnp.exp(jnp.where(causal, diff, 0.0)[None] * log_gamma[:, None, None]), 0.0)
    kdec = jnp.exp(log_gamma[:, None] * (cr - 1 - idx)[None, :])[:, :, None]
    qdec = jnp.exp(log_gamma[:, None] * (idx + 1.0)[None, :])[:, :, None]
    gam = jnp.broadcast_to(jnp.exp(cr * log_gamma)[:, None, None], (RET_HEADS, 1, LANES))
    dmat = _pad_rows(_pad_rows(dmat, TILE, 1), TILE, 2)
    return dmat, _pad_rows(qdec, TILE, 1), _pad_rows(kdec, TILE, 1), gam


def _pad_lanes(v):
    return jnp.pad(v, (0, LANES - v.shape[0]))[None, :]


def _tiles(n):
    return dict(tm_ffn=min(1024, n), tf=512, tm_out=min(512, n))


def _trunk(x, pos, rs0, gs0, cv0, p, emit_vn):
    b, t, d = x.shape
    n = b * t
    tl = _tiles(n)
    valid = min(t, TILE)
    tpad = -(-t // TILE) * TILE
    cosf, sinf = _rope_tables(pos)
    cosf, sinf = _pad_rows(cosf, tpad, 0), _pad_rows(sinf, tpad, 0)
    consts = _ret_consts(valid)
    x = x.reshape(n, d)
    outs = []
    for l in range(DEPTH):
        x = _ffn(x, p["norm_ffn1_w"][l][None], p["ffn1_w_up"], p["ffn1_w_down"], l, tl["tm_ffn"], tl["tf"])
        lp = {
            "alog": _pad_lanes(p["gdn_a_log"][l]), "dtb": _pad_lanes(p["gdn_dt_bias"][l]),
            "convw": p["gdn_conv_w"][l], "rgn": p["ret_gn_w"][l][None], "gnw": p["gdn_norm_w"][l][None],
            "mnw": p["gmlp_norm_w"][l][None],
            "ws": _pad_rows(_pad_rows(p["gmlp_ws"][l][:, :valid, :valid], TILE, 1), TILE, 2),
            "bs": _pad_rows(p["gmlp_bs"][l][:, :valid, None], TILE, 1),
        }
        res = _mixer(_pad_rows(x.reshape(b, t, d), tpad, 1), p["norm_mix_w"][l][None], p["w_in"], l,
                     cosf, sinf, consts, lp, rs0[l], gs0[l], cv0[l], valid, emit_vn)
        cat = res[0][:, :t].reshape(n, D_MIX)
        x = _outproj(x, cat, p["w_out"], l, tl["tm_out"])
        x = _ffn(x, p["norm_ffn2_w"][l][None], p["ffn2_w_up"], p["ffn2_w_down"], l, tl["tm_ffn"], tl["tf"],
                 final_nw=p["final_norm_w"][None] if l == DEPTH - 1 else None)
        outs.append(tuple(res[1:4]) + ((res[4][:, :t],) if emit_vn else ()))
    y = x.reshape(b, t, d)
    return (y,) + tuple(jnp.stack([o[i] for o in outs]) for i in range(len(outs[0])))


def _win_kernel(w_ref, o_ref, flat_ref):
    ab0 = OFF_CU
    nab = 2 * GDN_HEADS
    flat_ref[:, :ab0] = w_ref[:, :ab0].astype(BF16)
    flat_ref[:, OFF_CU:OFF_AB] = w_ref[:, ab0 + nab:].astype(BF16)
    flat_ref[:, OFF_AB:] = jnp.zeros((flat_ref.shape[0], D_PROJ - OFF_AB), BF16)
    flat_ref[:, OFF_AB:OFF_AB + nab] = w_ref[:, ab0:ab0 + nab].astype(BF16)
    for c in range(D_PROJ // PROJ_COLS):
        o_ref[c] = flat_ref[:, PROJ_COLS * c:PROJ_COLS * (c + 1)]


def _relayout_w_in(w, tr=256):
    depth, d, d_in = w.shape
    return pl.pallas_call(
        _win_kernel,
        grid=(depth, d // tr),
        in_specs=[pl.BlockSpec((None, tr, d_in), lambda l, i: (l, i, 0))],
        out_specs=pl.BlockSpec((None, D_PROJ // PROJ_COLS, tr, PROJ_COLS), lambda l, i: (l, 0, i, 0)),
        out_shape=jax.ShapeDtypeStruct((depth, D_PROJ // PROJ_COLS, d, PROJ_COLS), BF16),
        scratch_shapes=[pltpu.VMEM((tr, D_PROJ), BF16)],
        compiler_params=_cparams(("parallel", "parallel")),
        name="w_in_layout",
    )(w)


def _prep_params(p):
    q = dict(p)
    for k in ("ffn1_w_up", "ffn1_w_down", "ffn2_w_up", "ffn2_w_down", "w_out"):
        q[k] = p[k].astype(BF16)
    q["w_in"] = _relayout_w_in(p["w_in"])
    return q


def kernel(x_prompt, x_sample, state_ret, state_gdn, state_conv, norm_ffn1_w, ffn1_w_up, ffn1_w_down,
           norm_mix_w, w_in, ret_gn_w, gdn_conv_w, gdn_a_log, gdn_dt_bias, gdn_norm_w, gmlp_ws, gmlp_bs,
           gmlp_norm_w, w_out, norm_ffn2_w, ffn2_w_up, ffn2_w_down, final_norm_w):
    p = _prep_params({
        "norm_ffn1_w": norm_ffn1_w, "ffn1_w_up": ffn1_w_up, "ffn1_w_down": ffn1_w_down,
        "norm_mix_w": norm_mix_w, "w_in": w_in, "ret_gn_w": ret_gn_w, "gdn_conv_w": gdn_conv_w,
        "gdn_a_log": gdn_a_log, "gdn_dt_bias": gdn_dt_bias, "gdn_norm_w": gdn_norm_w,
        "gmlp_ws": gmlp_ws, "gmlp_bs": gmlp_bs, "gmlp_norm_w": gmlp_norm_w, "w_out": w_out,
        "norm_ffn2_w": norm_ffn2_w, "ffn2_w_up": ffn2_w_up, "ffn2_w_down": ffn2_w_down,
        "final_norm_w": final_norm_w})
    bp, tp = x_prompt.shape[0], x_prompt.shape[1]
    bs, ts = x_sample.shape[0], x_sample.shape[1]
    ret0 = jnp.zeros((DEPTH, bp, RET_HEADS, RET_DK, RET_DV), F32)
    gdn0 = jnp.zeros((DEPTH, bp, GDN_HEADS, GDN_DK, GDN_DV), F32)
    conv0 = jnp.zeros((DEPTH, bp, CONV_WIDTH - 1, GDN_CONV_CH), F32)
    pos_p = jnp.arange(tp, dtype=jnp.int32)
    y_p, ret_p, gdn_p, conv_p = _trunk(x_prompt, pos_p, ret0, gdn0, conv0, p, emit_vn=False)
    pos_s = PAST_LEN + jnp.arange(ts, dtype=jnp.int32)
    y_s, ret_s, gdn_s, conv_s, vn_s = _trunk(x_sample, pos_s, state_ret, state_gdn, state_conv, p, emit_vn=True)
    return (y_p, y_s, ret_p, ret_s, gdn_p, gdn_s, conv_p, conv_s, vn_s)
```
